```python
import math
import jax, jax.numpy as jnp
from jax import lax
import numpy as np

D_MODEL = 1024
BATCH = 2
SEQ = 8192
DEPTH = 2
DEC_BATCH = 32
DEC_SEQ = 8
PAST_LEN = 8192
PAGE_SIZE = 128

N_HEADS = 16
HEAD_DIM = 64
ATTN_WIDTH = N_HEADS * HEAD_DIM
IDX_HEADS = 8
IDX_DIM = 64
TOPK_MAX = 256
Q_BLOCK = 128
IN_COLS = 3 * ATTN_WIDTH + IDX_HEADS * IDX_DIM + IDX_DIM + IDX_HEADS
N_BUCKETS = 32
MAX_DISTANCE = 128
LRU_WIDTH = 1024
LRU_BLOCKS = 8
LRU_BLOCK = LRU_WIDTH // LRU_BLOCKS
CONV_WIDTH = 4
LRU_C = 8.0
D_FF = 2816
RMS_EPS = 1e-6

kernel_name = "hybrid_dsa_rglru_macaron_step"


def rmsnorm(x, g):
    x32 = x.astype(jnp.float32)
    y = x32 * lax.rsqrt(jnp.mean(x32 * x32, axis=-1, keepdims=True) + RMS_EPS)
    return (y * g.astype(jnp.float32)).astype(x.dtype)


def swiglu(h, w_gu, w_down):
    gu = h @ w_gu
    return (jax.nn.silu(gu[..., :D_FF]) * gu[..., D_FF:]) @ w_down


def t5_bucket(rel):
    n = jnp.maximum(rel, 0)
    max_exact = N_BUCKETS // 2
    nf = jnp.maximum(n, max_exact).astype(jnp.float32)
    large = max_exact + (jnp.log(nf / max_exact) / math.log(MAX_DISTANCE / max_exact)
                         * (N_BUCKETS - max_exact)).astype(jnp.int32)
    large = jnp.minimum(large, N_BUCKETS - 1)
    return jnp.where(n < max_exact, n, large)


def attn_project(h, w_in):
    n, t, _ = h.shape
    z = h @ w_in
    a = ATTN_WIDTH
    qi_end = 3 * a + IDX_HEADS * IDX_DIM
    q = z[..., :a].reshape(n, t, N_HEADS, HEAD_DIM)
    k = z[..., a:2 * a].reshape(n, t, N_HEADS, HEAD_DIM)
    v = z[..., 2 * a:3 * a].reshape(n, t, N_HEADS, HEAD_DIM)
    qi = z[..., 3 * a:qi_end].reshape(n, t, IDX_HEADS, IDX_DIM)
    ki = z[..., qi_end:qi_end + IDX_DIM]
    wi = z[..., qi_end + IDX_DIM:]
    return q, k, v, qi, ki, wi


def indexer_scores(qi, wi, ki):
    s = jnp.einsum('nqhd,nld->nqhl', qi, ki, preferred_element_type=jnp.float32) * (IDX_DIM ** -0.5)
    return jnp.einsum('nqhl,nqh->nql', jax.nn.relu(s), wi.astype(jnp.float32)) * (IDX_HEADS ** -0.5)


def select_keys(scores, q_pos, k_pos, n_sel):
    admissible = k_pos[None, None, :] <= q_pos[None, :, None]
    scores = jnp.where(admissible, scores, -jnp.inf)
    _, idx = lax.top_k(scores, n_sel)
    valid = idx <= q_pos[None, :, None]
    return idx, valid


def sparse_attend(q, k_sel, v_sel, idx, valid, q_pos, rel_bias):
    logits = jnp.einsum('nqhd,nqkhd->nqhk', q, k_sel, preferred_element_type=jnp.float32) * (HEAD_DIM ** -0.5)
    bias = rel_bias[t5_bucket(q_pos[None, :, None] - idx)]
    logits = logits + jnp.swapaxes(bias, -1, -2).astype(jnp.float32)
    logits = jnp.where(valid[:, :, None, :], logits, -jnp.inf)
    p = jax.nn.softmax(logits, axis=-1)
    return jnp.einsum('nqhk,nqkhd->nqhd', p.astype(v_sel.dtype), v_sel)


def attn_prompt(h, w_in, w_out, rel_bias):
    n, s_len, _ = h.shape
    q, k, v, qi, ki, wi = attn_project(h, w_in)
    n_sel = min(TOPK_MAX, s_len // 4)
    k_pos = jnp.arange(s_len, dtype=jnp.int32)
    bidx = jnp.arange(n)[:, None, None]

    def block(bi):
        s0 = bi * Q_BLOCK
        q_b = lax.dynamic_slice_in_dim(q, s0, Q_BLOCK, axis=1)
        qi_b = lax.dynamic_slice_in_dim(qi, s0, Q_BLOCK, axis=1)
        wi_b = lax.dynamic_slice_in_dim(wi, s0, Q_BLOCK, axis=1)
        q_pos = s0 + jnp.arange(Q_BLOCK, dtype=jnp.int32)
        idx, valid = select_keys(indexer_scores(qi_b, wi_b, ki), q_pos, k_pos, n_sel)
        k_sel = k[bidx, idx]
        v_sel = v[bidx, idx]
        return sparse_attend(q_b, k_sel, v_sel, idx, valid, q_pos, rel_bias)

    o = lax.map(block, jnp.arange(s_len // Q_BLOCK, dtype=jnp.int32))
    o = jnp.moveaxis(o, 0, 1).reshape(n, s_len, ATTN_WIDTH)
    return o @ w_out, k, v, ki


def attn_sample(h, cache_k, cache_v, cache_kidx, page_table, w_in, w_out, rel_bias):
    n, t, _ = h.shape
    past = page_table.shape[1] * PAGE_SIZE
    n_keys = past + t
    n_sel = min(TOPK_MAX, n_keys // 4)
    q, k, v, qi, ki, wi = attn_project(h, w_in)
    ki_past = cache_kidx[page_table].reshape(n, past, IDX_DIM)
    ki_all = jnp.concatenate([ki_past, ki.astype(ki_past.dtype)], axis=1)
    q_pos = past + jnp.arange(t, dtype=jnp.int32)
    k_pos = jnp.arange(n_keys, dtype=jnp.int32)
    idx, valid = select_keys(indexer_scores(qi, wi, ki_all), q_pos, k_pos, n_sel)
    bidx = jnp.arange(n)[:, None, None]
    past_idx = jnp.minimum(idx, past - 1)
    phys = page_table[bidx, past_idx // PAGE_SIZE]
    off = past_idx % PAGE_SIZE
    new_idx = jnp.clip(idx - past, 0, t - 1)
    is_past = (idx < past)[..., None, None]
    k_sel = jnp.where(is_past, cache_k[phys, off], k[bidx, new_idx])
    v_sel = jnp.where(is_past, cache_v[phys, off], v[bidx, new_idx])
    o = sparse_attend(q, k_sel, v_sel, idx, valid, q_pos, rel_bias).reshape(n, t, ATTN_WIDTH)
    return o @ w_out, k, v, ki


def rglru_mixer(h, conv_state, h0, w_in, conv_w, conv_b, w_a, b_a, w_x, b_x, lam, w_out):
    n, t, _ = h.shape
    z = h @ w_in
    gate = jax.nn.gelu(z[..., :LRU_WIDTH], approximate=True)
    xr = z[..., LRU_WIDTH:]
    xpad = jnp.concatenate([conv_state.astype(xr.dtype), xr], axis=1)
    xc = conv_b
    for j in range(CONV_WIDTH):
        xc = xc + xpad[:, j:j + t] * conv_w[j]
    new_conv = xpad[:, t:]
    xb = xc.reshape(n, t, LRU_BLOCKS, LRU_BLOCK)
    r = jax.nn.sigmoid(jnp.einsum('ntgi,gij->ntgj', xb, w_a) + b_a).reshape(n, t, LRU_WIDTH)
    ig = jax.nn.sigmoid(jnp.einsum('ntgi,gij->ntgj', xb, w_x) + b_x).reshape(n, t, LRU_WIDTH)
    log_a = -LRU_C * r.astype(jnp.float32) * jax.nn.softplus(-lam.astype(jnp.float32))
    a = jnp.exp(log_a)
    u = jnp.sqrt(-jnp.expm1(2.0 * log_a)) * (ig * xc).astype(jnp.float32)

    def step(hc, au):
        hc = au[0] * hc + au[1]
        return hc, hc

    h_last, hs = lax.scan(step, h0.astype(jnp.float32), (jnp.swapaxes(a, 0, 1), jnp.swapaxes(u, 0, 1)))
    y = jnp.swapaxes(hs, 0, 1).astype(h.dtype) * gate
    return y @ w_out, new_conv, h_last.astype(h0.dtype)


def setup_inputs(seed: int = 0) -> dict:
    key = jax.random.key(seed)
    ks = jax.random.split(key, 24)
    n_pages = PAST_LEN // PAGE_SIZE
    n_pool = (5 * DEC_BATCH * n_pages + 3) // 4
    f32 = jnp.float32
    nrm = lambda k, shape, scale: jax.random.normal(k, shape, f32) * scale
    perm = jax.random.permutation(ks[0], n_pool)[:DEC_BATCH * n_pages]
    page_table = perm.reshape(DEC_BATCH, n_pages).astype(jnp.int32)
    u = jax.random.uniform(ks[1], (LRU_WIDTH,), f32, minval=0.9, maxval=0.999)
    return {
        "x_prompt": nrm(ks[2], (BATCH, SEQ, D_MODEL), 1.0),
        "x_sample": nrm(ks[3], (DEC_BATCH, DEC_SEQ, D_MODEL), 1.0),
        "cache_k": nrm(ks[4], (n_pool, PAGE_SIZE, N_HEADS, HEAD_DIM), 1.0),
        "cache_v": nrm(ks[5], (n_pool, PAGE_SIZE, N_HEADS, HEAD_DIM), 1.0),
        "cache_kidx": nrm(ks[6], (n_pool, PAGE_SIZE, IDX_DIM), 1.0),
        "page_table": page_table,
        "state_conv": nrm(ks[7], (DEC_BATCH, CONV_WIDTH - 1, LRU_WIDTH), 1.0),
        "state_h": nrm(ks[8], (DEC_BATCH, LRU_WIDTH), 0.5),
        "rel_bias": nrm(ks[9], (N_BUCKETS, N_HEADS), 0.5),
        "attn_w_in": nrm(ks[10], (D_MODEL, IN_COLS), D_MODEL ** -0.5),
        "attn_w_out": nrm(ks[11], (ATTN_WIDTH, D_MODEL), ATTN_WIDTH ** -0.5),
        "lru_w_in": nrm(ks[12], (D_MODEL, 2 * LRU_WIDTH), D_MODEL ** -0.5),
        "lru_conv_w": nrm(ks[13], (CONV_WIDTH, LRU_WIDTH), CONV_WIDTH ** -0.5),
        "lru_conv_b": nrm(ks[14], (LRU_WIDTH,), 0.01),
        "lru_w_a": nrm(ks[15], (LRU_BLOCKS, LRU_BLOCK, LRU_BLOCK), LRU_BLOCK ** -0.5),
        "lru_b_a": nrm(ks[16], (LRU_BLOCKS, LRU_BLOCK), 0.01),
        "lru_w_x": nrm(ks[17], (LRU_BLOCKS, LRU_BLOCK, LRU_BLOCK), LRU_BLOCK ** -0.5),
        "lru_b_x": nrm(ks[18], (LRU_BLOCKS, LRU_BLOCK), 0.01),
        "lru_lambda": jnp.log(u) - jnp.log1p(-u),
        "lru_w_out": nrm(ks[19], (LRU_WIDTH, D_MODEL), LRU_WIDTH ** -0.5),
        "norm_g": 1.0 + nrm(ks[20], (DEPTH, 3, D_MODEL), 0.01),
        "ffn_w_gu": nrm(ks[21], (DEPTH, 2, D_MODEL, 2 * D_FF), D_MODEL ** -0.5),
        "ffn_w_down": nrm(ks[22], (DEPTH, 2, D_FF, D_MODEL), D_FF ** -0.5),
        "final_norm_g": 1.0 + nrm(ks[23], (D_MODEL,), 0.01),
    }


def reference(x_prompt, x_sample, cache_k, cache_v, cache_kidx, page_table, state_conv, state_h,
              rel_bias, attn_w_in, attn_w_out, lru_w_in, lru_conv_w, lru_conv_b, lru_w_a, lru_b_a,
              lru_w_x, lru_b_x, lru_lambda, lru_w_out, norm_g, ffn_w_gu, ffn_w_down, final_norm_g):
    xp, xs = x_prompt, x_sample
    lru_params = (lru_w_in, lru_conv_w, lru_conv_b, lru_w_a, lru_b_a, lru_w_x, lru_b_x, lru_lambda, lru_w_out)
    for i in range(DEPTH):
        xp = xp + 0.5 * swiglu(rmsnorm(xp, norm_g[i, 0]), ffn_w_gu[i, 0], ffn_w_down[i, 0])
        xs = xs + 0.5 * swiglu(rmsnorm(xs, norm_g[i, 0]), ffn_w_gu[i, 0], ffn_w_down[i, 0])
        hp = rmsnorm(xp, norm_g[i, 1])
        hs = rmsnorm(xs, norm_g[i, 1])
        if i % 2 == 0:
            mp, k_p, v_p, ki_p = attn_prompt(hp, attn_w_in, attn_w_out, rel_bias)
            ms, k_s, v_s, ki_s = attn_sample(hs, cache_k, cache_v, cache_kidx, page_table,
                                             attn_w_in, attn_w_out, rel_bias)
        else:
            zc = jnp.zeros((xp.shape[0], CONV_WIDTH - 1, LRU_WIDTH), xp.dtype)
            zh = jnp.zeros((xp.shape[0], LRU_WIDTH), xp.dtype)
            mp, conv_p, h_p = rglru_mixer(hp, zc, zh, *lru_params)
            ms, conv_s, h_s = rglru_mixer(hs, state_conv, state_h, *lru_params)
        xp = xp + mp
        xs = xs + ms
        xp = xp + 0.5 * swiglu(rmsnorm(xp, norm_g[i, 2]), ffn_w_gu[i, 1], ffn_w_down[i, 1])
        xs = xs + 0.5 * swiglu(rmsnorm(xs, norm_g[i, 2]), ffn_w_gu[i, 1], ffn_w_down[i, 1])
    y_prompt = rmsnorm(xp, final_norm_g)
    y_sample = rmsnorm(xs, final_norm_g)
    return (y_prompt, y_sample, k_p, v_p, ki_p, k_s, v_s, ki_s, conv_p, h_p, conv_s, h_s)
```

```python
import functools
import math

import numpy as np
import jax
import jax.numpy as jnp
from jax import lax
from jax.experimental import pallas as pl
from jax.experimental.pallas import tpu as pltpu

F32 = jnp.float32
BF16 = jnp.bfloat16
NEG_INF = float("-inf")

N_HEADS = 16
HEAD_DIM = 64
ATTN_WIDTH = N_HEADS * HEAD_DIM
IDX_HEADS = 8
IDX_DIM = 64
TOPK_MAX = 256
Q_BLOCK = 128
PAGE_SIZE = 128
N_BUCKETS = 32
MAX_DISTANCE = 128
LRU_BLOCKS = 8
CONV_WIDTH = 4
LRU_C = 8.0
RMS_EPS = 1e-6

V7X_VMEM_BYTES = 64 * 1024 * 1024
LANES = 128
SUBLANES = 8

N_PAIRS = N_HEADS // 2
SCORE_CHUNK = 512
WIDE_CHUNK = 512
MAX_BISECT_ITERS = 320


def _vmem_limit(nbytes):
    return int(min(nbytes, V7X_VMEM_BYTES - 4 * 1024 * 1024))


def _rms(x, g):
    ms = jnp.mean(x * x, axis=-1, keepdims=True)
    return x * lax.rsqrt(ms + RMS_EPS) * g


def _sigmoid(x):
    return 1.0 / (1.0 + jnp.exp(-x))


def _dot(a, b):
    return jnp.dot(a, b, preferred_element_type=F32)


def _dot_nt(a, b):
    return lax.dot_general(a, b, (((1,), (1,)), ((), ())), preferred_element_type=F32)


def _fold_lanes(x, op):
    out = x[:, :LANES]
    for j in range(1, x.shape[1] // LANES):
        out = op(out, x[:, j * LANES:(j + 1) * LANES])
    return out


def _ffn_body(*refs, d_ff, fc, final):
    if final:
        x_ref, g_ref, wgu_ref, wd_ref, gf_ref, o_ref = refs
    else:
        x_ref, g_ref, wgu_ref, wd_ref, o_ref = refs
    x = x_ref[...]
    h = _rms(x, g_ref[...]).astype(BF16)
    acc = jnp.zeros(x.shape, F32)
    for c in range(d_ff // fc):
        gg = _dot(h, wgu_ref[:, c * fc:(c + 1) * fc])
        uu = _dot(h, wgu_ref[:, d_ff + c * fc:d_ff + (c + 1) * fc])
        a = (gg * _sigmoid(gg) * uu).astype(BF16)
        acc = acc + _dot(a, wd_ref[c * fc:(c + 1) * fc, :])
    y = x + 0.5 * acc
    if final:
        y = _rms(y, gf_ref[...])
    o_ref[...] = y


def _ffn(x, g, wgu, wd, gf=None):
    t, d = x.shape
    d_ff = wd.shape[0]
    tm = 512 if t % 512 == 0 else 256
    fc = 256
    final = gf is not None
    const = lambda i: (0, 0)
    in_specs = [
        pl.BlockSpec((tm, d), lambda i: (i, 0)),
        pl.BlockSpec((1, d), const),
        pl.BlockSpec((d, 2 * d_ff), const, pipeline_mode=pl.Buffered(1)),
        pl.BlockSpec((d_ff, d), const, pipeline_mode=pl.Buffered(1)),
    ]
    args = [x, g.reshape(1, d), wgu, wd]
    if final:
        in_specs.append(pl.BlockSpec((1, d), const))
        args.append(gf.reshape(1, d))
    return pl.pallas_call(
        functools.partial(_ffn_body, d_ff=d_ff, fc=fc, final=final),
        grid=(t // tm,),
        in_specs=in_specs,
        out_specs=pl.BlockSpec((tm, d), lambda i: (i, 0)),
        out_shape=jax.ShapeDtypeStruct((t, d), F32),
        compiler_params=pltpu.CompilerParams(
            dimension_semantics=("arbitrary",),
            vmem_limit_bytes=_vmem_limit(48 * 1024 * 1024)),
        name="ffn_final" if final else "ffn",
    )(*args)


def _proj_body(x_ref, g_ref, wqkv_ref, wqi_ref, wkw_ref,
               q_ref, k_ref, v_ref, kb_ref, vb_ref, qi_ref, kw_ref):
    h = _rms(x_ref[...], g_ref[...]).astype(BF16)
    a = ATTN_WIDTH
    q_ref[...] = (_dot(h, wqkv_ref[:, :a]) * (HEAD_DIM ** -0.5)).astype(BF16)
    k = _dot(h, wqkv_ref[:, a:2 * a])
    k_ref[...] = k
    kb_ref[...] = k.astype(BF16)
    v = _dot(h, wqkv_ref[:, 2 * a:])
    v_ref[...] = v
    vb_ref[...] = v.astype(BF16)
    qi_ref[...] = (_dot(h, wqi_ref[...]) * (IDX_DIM ** -0.5)).astype(BF16)
    kw_ref[...] = _dot(h, wkw_ref[...])


def _attn_proj(x, g, wqkv, wqi, wkw):
    t, d = x.shape
    tm = 256
    a = ATTN_WIDTH
    nqi = IDX_HEADS * IDX_DIM
    const = lambda i: (0, 0)
    row = lambda i: (i, 0)
    return pl.pallas_call(
        _proj_body,
        grid=(t // tm,),
        in_specs=[
            pl.BlockSpec((tm, d), row),
            pl.BlockSpec((1, d), const),
            pl.BlockSpec((d, 3 * a), const, pipeline_mode=pl.Buffered(1)),
            pl.BlockSpec((d, nqi), const, pipeline_mode=pl.Buffered(1)),
            pl.BlockSpec((d, LANES), const, pipeline_mode=pl.Buffered(1)),
        ],
        out_specs=[
            pl.BlockSpec((tm, a), row), pl.BlockSpec((tm, a), row), pl.BlockSpec((tm, a), row),
            pl.BlockSpec((tm, a), row), pl.BlockSpec((tm, a), row),
            pl.BlockSpec((tm, nqi), row), pl.BlockSpec((tm, LANES), row),
        ],
        out_shape=[
            jax.ShapeDtypeStruct((t, a), BF16), jax.ShapeDtypeStruct((t, a), F32),
            jax.ShapeDtypeStruct((t, a), F32), jax.ShapeDtypeStruct((t, a), BF16),
            jax.ShapeDtypeStruct((t, a), BF16), jax.ShapeDtypeStruct((t, nqi), BF16),
            jax.ShapeDtypeStruct((t, LANES), F32),
        ],
        compiler_params=pltpu.CompilerParams(
            dimension_semantics=("arbitrary",),
            vmem_limit_bytes=_vmem_limit(40 * 1024 * 1024)),
        name="attn_proj",
    )(x, g.reshape(1, d), wqkv, wqi, wkw)


def _select_topk(count, smin, smax, n_adm, n_sel, n_cols):
    ksel = float(n_sel)
    c_hi = count(lambda s, col: s >= smax)
    few = n_adm <= ksel
    top_tied = jnp.logical_and(jnp.logical_not(few), c_hi >= ksel)
    lo0 = jnp.where(top_tied, smax, smin)
    done0 = jnp.where(jnp.logical_or(few, top_tied), 1.0, 0.0)

    def not_all_done(done):
        return (jnp.min(done) < 0.5).astype(jnp.int32)

    def cond(carry):
        return jnp.logical_and(carry[3] > 0, carry[4] < MAX_BISECT_ITERS)

    def body(carry):
        lo, hi, done, _, it = carry
        mid = lo * 0.5 + hi * 0.5
        conv = jnp.logical_or(mid <= lo, mid >= hi)
        c = count(lambda s, col: s >= mid)
        ge = c >= ksel
        upd = jnp.logical_and(done < 0.5, jnp.logical_not(conv))
        lo = jnp.where(jnp.logical_and(upd, ge), mid, lo)
        hi = jnp.where(jnp.logical_and(upd, jnp.logical_not(ge)), mid, hi)
        done = jnp.where(jnp.logical_or(conv, c == ksel), 1.0, done)
        return lo, hi, done, not_all_done(done), it + 1

    thr = lax.while_loop(cond, body, (lo0, smax, done0, not_all_done(done0), jnp.int32(0)))[0]

    c_ge = count(lambda s, col: s >= thr)
    has_tie = c_ge > ksel
    big = float(n_cols)

    def tie_break(_):
        c_gt = count(lambda s, col: s > thr)
        need = ksel - c_gt
        jlo = jnp.full_like(thr, -1.0)
        jhi = jnp.full_like(thr, big)

        def jbody(_, jc):
            jlo, jhi = jc
            jm = jnp.floor((jlo + jhi) * 0.5)
            c = count(lambda s, col: jnp.logical_and(s == thr, col <= jm))
            ok = c >= need
            return jnp.where(ok, jlo, jm), jnp.where(ok, jm, jhi)

        n_it = int(math.ceil(math.log2(n_cols + 2))) + 1
        _, jhi = lax.fori_loop(0, n_it, jbody, (jlo, jhi))
        return jnp.where(has_tie, jhi, big)

    any_tie = jnp.max(jnp.where(has_tie, 1.0, 0.0)) > 0.5
    jmax = lax.cond(any_tie, tie_break, lambda _: jnp.full_like(thr, big), 0)
    return thr, jmax


def _t5_bucket_np(rel):
    n = np.maximum(rel, 0)
    max_exact = N_BUCKETS // 2
    nf = np.maximum(n, max_exact).astype(np.float32)
    large = max_exact + (np.log(nf / np.float32(max_exact)) / np.float32(math.log(MAX_DISTANCE / max_exact))
                         * (N_BUCKETS - max_exact)).astype(np.int32)
    large = np.minimum(large, N_BUCKETS - 1)
    return np.where(n < max_exact, n, large)


_FAR_BUCKET = int(_t5_bucket_np(np.array([1 << 20]))[0])
assert int(_t5_bucket_np(np.array([Q_BLOCK + 1]))[0]) == _FAR_BUCKET


def _bias_rows(rel_bias, rel):
    bucket = _t5_bucket_np(rel)
    b = jnp.take(rel_bias, jnp.asarray(bucket.reshape(-1)), axis=0).reshape(rel.shape + (N_HEADS,))
    b = b - rel_bias[_FAR_BUCKET][None, None, :]
    return jnp.transpose(b, (2, 0, 1))


def _attn_prompt_body(qi_ref, kw_ref, kit_ref, q_ref, kt_ref, v_ref, toe_ref, x_ref, wo_ref,
                      o_ref, s_ref, lg_ref, oacc_ref, *, seq, n_sel):
    qb = pl.program_id(1)
    qn = Q_BLOCK
    sc = SCORE_CHUNK
    n_sc = (qb + sc // qn) // (sc // qn)

    qi = qi_ref[0, 0]
    wi = kw_ref[:, IDX_DIM:IDX_DIM + IDX_HEADS] * (IDX_HEADS ** -0.5)
    wcols = [jnp.broadcast_to(wi[:, h:h + 1], (qn, sc)) for h in range(IDX_HEADS)]
    qpos = lax.broadcasted_iota(jnp.int32, (qn, sc), 0) + qb * qn
    lane = lax.broadcasted_iota(jnp.int32, (qn, sc), 1)

    def score_chunk(c, carry):
        smin, smax = carry
        c0 = pl.multiple_of(c * sc, sc)
        s = _dot(qi, kit_ref[0, :, pl.ds(c0, sc)])
        tot = jnp.maximum(s[:qn], 0.0) * wcols[0]
        for h in range(1, IDX_HEADS):
            tot = tot + jnp.maximum(s[h * qn:(h + 1) * qn], 0.0) * wcols[h]
        adm = (lane + c0) <= qpos
        s_ref[:, pl.ds(c0, sc)] = jnp.where(adm, tot, NEG_INF)
        smin = jnp.minimum(smin, _fold_lanes(jnp.where(adm, tot, jnp.inf), jnp.minimum))
        smax = jnp.maximum(smax, _fold_lanes(jnp.where(adm, tot, NEG_INF), jnp.maximum))
        return smin, smax

    smin, smax = lax.fori_loop(
        0, n_sc, score_chunk,
        (jnp.full((qn, LANES), jnp.inf, F32), jnp.full((qn, LANES), NEG_INF, F32)))
    smin = jnp.min(smin, axis=1, keepdims=True)
    smax = jnp.max(smax, axis=1, keepdims=True)

    def count(pred):
        def body(c, acc):
            c0 = pl.multiple_of(c * sc, sc)
            s = s_ref[:, pl.ds(c0, sc)]
            col = (lane + c0).astype(F32)
            return acc + _fold_lanes(jnp.where(pred(s, col), 1.0, 0.0), jnp.add)
        acc = lax.fori_loop(0, n_sc, body, jnp.zeros((qn, LANES), F32))
        return jnp.sum(acc, axis=1, keepdims=True)

    n_adm = (lax.broadcasted_iota(jnp.int32, (qn, 1), 0) + qb * qn + 1).astype(F32)
    thr, jmax = _select_topk(count, smin, smax, n_adm, n_sel, seq)

    def mask_chunk(c, _):
        c0 = pl.multiple_of(c * sc, sc)
        s = s_ref[:, pl.ds(c0, sc)]
        col = (lane + c0).astype(F32)
        sel = jnp.logical_or(s > thr, jnp.logical_and(s == thr, col <= jmax))
        sel = jnp.logical_and(sel, s > NEG_INF)
        s_ref[:, pl.ds(c0, sc)] = jnp.where(sel, 0.0, NEG_INF)
        return 0

    lax.fori_loop(0, n_sc, mask_chunk, 0)

    n_far = jnp.maximum(qb - 1, 0)
    wc = WIDE_CHUNK
    per = wc // qn
    lane_p = lax.broadcasted_iota(jnp.int32, (qn, LANES), 1)

    def pair_body(pr, _):
        p0 = pl.multiple_of(pr * LANES, LANES)
        qp = q_ref[:, pl.ds(p0, LANES)]
        zero = jnp.zeros_like(qp)
        qs = jnp.concatenate([jnp.where(lane_p < HEAD_DIM, qp, zero),
                              jnp.where(lane_p >= HEAD_DIM, qp, zero)], axis=0)

        def logits(c0, width):
            lg = _dot(qs, kt_ref[0, pl.ds(p0, LANES), pl.ds(c0, width)])
            m = s_ref[:, pl.ds(c0, width)]
            return lg + jnp.concatenate([m, m], axis=0)

        def p1_wide(c, mrun):
            c0 = pl.multiple_of(c * wc, wc)
            lg = logits(c0, wc)
            lg_ref[:, pl.ds(c0, wc)] = lg
            return jnp.maximum(mrun, _fold_lanes(lg, jnp.maximum))

        def p1_far_block(j, mrun):
            c0 = pl.multiple_of((n_far // per) * wc + j * qn, qn)
            lg = logits(c0, qn)
            lg_ref[:, pl.ds(c0, qn)] = lg
            return jnp.maximum(mrun, lg)

        def p1_near_block(j, mrun):
            kb = qb - 1 + j
            c0 = pl.multiple_of(kb * qn, qn)
            t0 = pl.multiple_of(j * qn, qn)
            lg = logits(c0, qn) + toe_ref[pr, :, pl.ds(t0, qn)]
            lg_ref[:, pl.ds(c0, qn)] = lg
            return jnp.maximum(mrun, lg)

        mrun = jnp.full((2 * qn, LANES), NEG_INF, F32)
        mrun = lax.fori_loop(0, n_far // per, p1_wide, mrun)
        mrun = lax.fori_loop(0, n_far % per, p1_far_block, mrun)
        mrun = lax.fori_loop(jnp.where(qb > 0, 0, 1), 2, p1_near_block, mrun)
        m = jnp.max(mrun, axis=1, keepdims=True)

        def p2(c0, width, carry):
            lrun, acc = carry
            p = jnp.exp(lg_ref[:, pl.ds(c0, width)] - m)
            lrun = lrun + _fold_lanes(p, jnp.add)
            acc = acc + _dot(p.astype(BF16), v_ref[0, pl.ds(c0, width), pl.ds(p0, LANES)])
            return lrun, acc

        def p2_wide(c, carry):
            return p2(pl.multiple_of(c * wc, wc), wc, carry)

        n_all = qb + 1

        def p2_block(j, carry):
            return p2(pl.multiple_of((n_all // per) * wc + j * qn, qn), qn, carry)

        carry = (jnp.zeros((2 * qn, LANES), F32), jnp.zeros((2 * qn, LANES), F32))
        carry = lax.fori_loop(0, n_all // per, p2_wide, carry)
        lrun, acc = lax.fori_loop(0, n_all % per, p2_block, carry)
        o = acc / jnp.sum(lrun, axis=1, keepdims=True)
        oacc_ref[:, pl.ds(p0, LANES)] = jnp.where(lane_p < HEAD_DIM, o[:qn], o[qn:])
        return 0

    lax.fori_loop(0, N_PAIRS, pair_body, 0)

    o_ref[...] = x_ref[...] + _dot(oacc_ref[...].astype(BF16), wo_ref[...])


def _attn_prompt(x, q, kt, vb, qi4, kw, kit, toe, wo, batch, seq):
    d = x.shape[1]
    a = ATTN_WIDTH
    nqb = seq // Q_BLOCK
    n_sel = min(TOPK_MAX, seq // 4)
    blk = lambda b, i: (b * nqb + i, 0)
    per_b3 = lambda b, i: (b, 0, 0)
    resident = 2 * seq * a * 2 + Q_BLOCK * seq * 4 + 2 * Q_BLOCK * seq * 4
    return pl.pallas_call(
        functools.partial(_attn_prompt_body, seq=seq, n_sel=n_sel),
        grid=(batch, nqb),
        in_specs=[
            pl.BlockSpec((1, 1, IDX_HEADS * Q_BLOCK, IDX_DIM), lambda b, i: (b, i, 0, 0)),
            pl.BlockSpec((Q_BLOCK, LANES), blk),
            pl.BlockSpec((1, IDX_DIM, seq), per_b3),
            pl.BlockSpec((Q_BLOCK, a), blk),
            pl.BlockSpec((1, a, seq), per_b3, pipeline_mode=pl.Buffered(1)),
            pl.BlockSpec((1, seq, a), per_b3, pipeline_mode=pl.Buffered(1)),
            pl.BlockSpec((N_PAIRS, 2 * Q_BLOCK, 2 * Q_BLOCK), lambda b, i: (0, 0, 0), pipeline_mode=pl.Buffered(1)),
            pl.BlockSpec((Q_BLOCK, d), blk),
            pl.BlockSpec((a, d), lambda b, i: (0, 0), pipeline_mode=pl.Buffered(1)),
        ],
        out_specs=pl.BlockSpec((Q_BLOCK, d), blk),
        out_shape=jax.ShapeDtypeStruct(x.shape, F32),
        scratch_shapes=[
            pltpu.VMEM((Q_BLOCK, seq), F32),
            pltpu.VMEM((2 * Q_BLOCK, seq), F32),
            pltpu.VMEM((Q_BLOCK, a), F32),
        ],
        compiler_params=pltpu.CompilerParams(
            dimension_semantics=("arbitrary", "arbitrary"),
            vmem_limit_bytes=_vmem_limit(resident + 14 * 1024 * 1024)),
        name="attn_prompt",
    )(qi4, kw, kit, q, kt, vb, toe, x, wo)


def _sample_scores_body(pt_ref, qi_ref, wb_ref, kidx_ref, kinew_ref, o_ref, *, n_pages, dec_seq):
    p = pl.program_id(1)
    is_past = p < n_pages
    kpage = jnp.where(is_past, kidx_ref[0], kinew_ref[0]).astype(BF16)
    s = _dot_nt(qi_ref[0], kpage)
    t = jnp.maximum(s, 0.0) * wb_ref[0]
    tot = t[:dec_seq]
    for h in range(1, IDX_HEADS):
        tot = tot + t[h * dec_seq:(h + 1) * dec_seq]
    j = lax.broadcasted_iota(jnp.int32, tot.shape, 1)
    tok = lax.broadcasted_iota(jnp.int32, tot.shape, 0)
    adm = jnp.logical_or(is_past, j <= tok)
    o_ref[0] = jnp.where(adm, tot, NEG_INF)


def _sample_scores(page_table, qi_s, wb, cache_kidx, ki_new_pad, dec_seq):
    n, n_pages = page_table.shape
    rows = IDX_HEADS * dec_seq
    grid_spec = pltpu.PrefetchScalarGridSpec(
        num_scalar_prefetch=1,
        grid=(n, n_pages + 1),
        in_specs=[
            pl.BlockSpec((1, rows, IDX_DIM), lambda i, p, pt: (i, 0, 0)),
            pl.BlockSpec((1, rows, LANES), lambda i, p, pt: (i, 0, 0)),
            pl.BlockSpec((1, PAGE_SIZE, IDX_DIM), lambda i, p, pt: (pt[i, jnp.minimum(p, n_pages - 1)], 0, 0)),
            pl.BlockSpec((1, PAGE_SIZE, IDX_DIM), lambda i, p, pt: (i, 0, 0)),
        ],
        out_specs=pl.BlockSpec((1, dec_seq, PAGE_SIZE), lambda i, p, pt: (i, 0, p)),
    )
    return pl.pallas_call(
        functools.partial(_sample_scores_body, n_pages=n_pages, dec_seq=dec_seq),
        grid_spec=grid_spec,
        out_shape=jax.ShapeDtypeStruct((n, dec_seq, (n_pages + 1) * PAGE_SIZE), F32),
        compiler_params=pltpu.CompilerParams(dimension_semantics=("arbitrary", "arbitrary")),
        name="sample_scores",
    )(page_table, qi_s, wb, cache_kidx, ki_new_pad)


def _attn_sample_body(pt_ref, sc_ref, q_ref, kpage_ref, vpage_ref, knew_ref, vnew_ref, bias_ref, x_ref, wo_ref,
                      o_ref, mask_ref, qbd_ref, m_ref, l_ref, acc_ref, *, n_pages, dec_seq, n_sel):
    p = pl.program_id(1)
    rows = N_HEADS * dec_seq
    n_cols = (n_pages + 1) * PAGE_SIZE

    @pl.when(p == 0)
    def _():
        s = sc_ref[0]
        col = lax.broadcasted_iota(jnp.int32, s.shape, 1).astype(F32)
        fin = s > NEG_INF
        smin = jnp.min(jnp.where(fin, s, jnp.inf), axis=1, keepdims=True)
        smax = jnp.max(s, axis=1, keepdims=True)

        def count(pred):
            return jnp.sum(jnp.where(pred(s, col), 1.0, 0.0), axis=1, keepdims=True)

        tok = lax.broadcasted_iota(jnp.int32, (dec_seq, 1), 0)
        n_adm = (tok + n_pages * PAGE_SIZE + 1).astype(F32)
        thr, jmax = _select_topk(count, smin, smax, n_adm, n_sel, n_cols)
        sel = jnp.logical_or(s > thr, jnp.logical_and(s == thr, col <= jmax))
        sel = jnp.logical_and(sel, fin)
        mask_ref[...] = jnp.where(sel, 0.0, NEG_INF)
        qt = jnp.concatenate([q_ref[0]] * N_HEADS, axis=0)
        r_head = lax.broadcasted_iota(jnp.int32, qt.shape, 0) // dec_seq
        c_head = lax.broadcasted_iota(jnp.int32, qt.shape, 1) // HEAD_DIM
        qbd_ref[...] = jnp.where(r_head == c_head, qt, 0.0).astype(BF16)
        m_ref[...] = jnp.full(m_ref.shape, NEG_INF, F32)
        l_ref[...] = jnp.zeros(l_ref.shape, F32)
        acc_ref[...] = jnp.zeros(acc_ref.shape, F32)

    is_past = p < n_pages
    kp = jnp.where(is_past, kpage_ref[0], knew_ref[0]).astype(BF16)
    vp = jnp.where(is_past, vpage_ref[0], vnew_ref[0]).astype(BF16)
    lg = _dot_nt(qbd_ref[...], kp)
    bidx = jnp.clip(p - (n_pages - 2), 0, 2)
    c0 = pl.multiple_of(p * PAGE_SIZE, PAGE_SIZE)
    mk = mask_ref[:, pl.ds(c0, PAGE_SIZE)]
    lg = lg + bias_ref[bidx] + jnp.concatenate([mk] * N_HEADS, axis=0)
    m_old = m_ref[...]
    m_new = jnp.maximum(m_old, jnp.max(lg, axis=1, keepdims=True))
    m_safe = jnp.where(m_new == NEG_INF, 0.0, m_new)
    alpha = jnp.exp(m_old - m_safe)
    pr = jnp.exp(lg - m_safe)
    l_ref[...] = alpha * l_ref[...] + jnp.sum(pr, axis=1, keepdims=True)
    acc_ref[...] = alpha * acc_ref[...] + _dot(pr.astype(BF16), vp)
    m_ref[...] = m_new

    @pl.when(p == n_pages)
    def _():
        o = acc_ref[...] / l_ref[...]
        r_head = lax.broadcasted_iota(jnp.int32, o.shape, 0) // dec_seq
        c_head = lax.broadcasted_iota(jnp.int32, o.shape, 1) // HEAD_DIM
        o = jnp.where(r_head == c_head, o, 0.0)
        tot = o[:dec_seq]
        for h in range(1, N_HEADS):
            tot = tot + o[h * dec_seq:(h + 1) * dec_seq]
        tot = jnp.concatenate([tot, jnp.zeros_like(tot)], axis=0).astype(BF16)
        o_ref[0] = x_ref[0] + _dot(tot, wo_ref[...])[:dec_seq]


def _attn_sample(page_table, scores, q_s, cache_k, cache_v, k_new_pad, v_new_pad, bias3, x_s, wo, dec_seq):
    n, n_pages = page_table.shape
    a = ATTN_WIDTH
    d = x_s.shape[-1]
    rows = N_HEADS * dec_seq
    n_cols = (n_pages + 1) * PAGE_SIZE
    n_sel = min(TOPK_MAX, (n_pages * PAGE_SIZE + dec_seq) // 4)
    per_seq = lambda i, p, pt: (i, 0, 0)
    page = lambda i, p, pt: (pt[i, jnp.minimum(p, n_pages - 1)], 0, 0)
    grid_spec = pltpu.PrefetchScalarGridSpec(
        num_scalar_prefetch=1,
        grid=(n, n_pages + 1),
        in_specs=[
            pl.BlockSpec((1, dec_seq, n_cols), per_seq),
            pl.BlockSpec((1, dec_seq, a), per_seq),
            pl.BlockSpec((1, PAGE_SIZE, a), page),
            pl.BlockSpec((1, PAGE_SIZE, a), page),
            pl.BlockSpec((1, PAGE_SIZE, a), per_seq),
            pl.BlockSpec((1, PAGE_SIZE, a), per_seq),
            pl.BlockSpec((3, rows, PAGE_SIZE), lambda i, p, pt: (0, 0, 0)),
            pl.BlockSpec((1, dec_seq, d), per_seq),
            pl.BlockSpec((a, d), lambda i, p, pt: (0, 0)),
        ],
        out_specs=pl.BlockSpec((1, dec_seq, d), per_seq),
        scratch_shapes=[
            pltpu.VMEM((dec_seq, n_cols), F32),
            pltpu.VMEM((rows, a), BF16),
            pltpu.VMEM((rows, 1), F32),
            pltpu.VMEM((rows, 1), F32),
            pltpu.VMEM((rows, a), F32),
        ],
    )
    return pl.pallas_call(
        functools.partial(_attn_sample_body, n_pages=n_pages, dec_seq=dec_seq, n_sel=n_sel),
        grid_spec=grid_spec,
        out_shape=jax.ShapeDtypeStruct(x_s.shape, F32),
        compiler_params=pltpu.CompilerParams(
            dimension_semantics=("arbitrary", "arbitrary"),
            vmem_limit_bytes=_vmem_limit(32 * 1024 * 1024)),
        name="attn_sample",
    )(page_table, scores, q_s, cache_k, cache_v, k_new_pad, v_new_pad, bias3, x_s, wo)


def _gelu_tanh(x):
    return 0.5 * x * (1.0 + jnp.tanh(math.sqrt(2.0 / math.pi) * (x + 0.044715 * (x * x * x))))


def _lru_body(*refs, sample, tm, width):
    if sample:
        (x_ref, g_ref, win_ref, cw_ref, cb_ref, wax_ref, bax_ref, lam_ref, wout_ref,
         r1_ref, r2_ref, r3_ref, h0_ref, o_ref, xr_ref, hs_ref, a_ref, u_ref) = refs
    else:
        (x_ref, g_ref, win_ref, cw_ref, cb_ref, wax_ref, bax_ref, lam_ref, wout_ref,
         o_ref, xr_ref, hs_ref, a_ref, u_ref, prev_ref, hc_ref) = refs

        @pl.when(pl.program_id(1) == 0)
        def _():
            prev_ref[...] = jnp.zeros(prev_ref.shape, F32)
            hc_ref[...] = jnp.zeros(hc_ref.shape, F32)

    w = width
    x = x_ref[...]
    h = _rms(x, g_ref[...]).astype(BF16)
    z = _dot(h, win_ref[...])
    gate = _gelu_tanh(z[:, :w])
    xr = z[:, w:]
    row = lax.broadcasted_iota(jnp.int32, (tm, w), 0)
    pos8 = row & (SUBLANES - 1)

    shifted = []
    for k in range(1, CONV_WIDTH):
        rolled = pltpu.roll(xr, k, 0)
        if sample:
            first = (r1_ref, r2_ref, r3_ref)[k - 1][...]
            shifted.append(jnp.where(pos8 < k, first, rolled))
        else:
            first = jnp.concatenate([pltpu.roll(prev_ref[...], k, 0)] * (tm // SUBLANES), axis=0)
            shifted.append(jnp.where(row < k, first, rolled))
    xc = cb_ref[...]
    for j in range(CONV_WIDTH - 1):
        xc = xc + shifted[CONV_WIDTH - 2 - j] * cw_ref[j:j + 1, :]
    xc = xc + xr * cw_ref[CONV_WIDTH - 1:CONV_WIDTH, :]

    lam = lam_ref[...]
    nl = -lam
    softplus = jnp.maximum(nl, 0.0) + jnp.log1p(jnp.exp(-jnp.abs(nl)))
    xcb = xc.astype(BF16)
    bw = w // LRU_BLOCKS
    for gi in range(LRU_BLOCKS):
        sl = slice(gi * bw, (gi + 1) * bw)
        ga = _dot(xcb[:, sl], wax_ref[gi]) + bax_ref[gi]
        r = _sigmoid(ga[:, :bw])
        ig = _sigmoid(ga[:, bw:])
        log_a = -LRU_C * r * softplus[:, sl]
        a_ref[:, sl] = jnp.exp(log_a)
        th = jnp.tanh(log_a)
        u_ref[:, sl] = jnp.sqrt(-2.0 * th / (1.0 - th)) * (ig * xc[:, sl])

    a_c = a_ref[...]
    u_c = u_ref[...]
    for s in (1, 2, 4):
        ok = pos8 >= s
        a_sh = jnp.where(ok, pltpu.roll(a_c, s, 0), 1.0)
        u_sh = jnp.where(ok, pltpu.roll(u_c, s, 0), 0.0)
        u_c = u_c + a_c * u_sh
        a_c = a_c * a_sh
    if sample:
        hs = u_c + a_c * h0_ref[...]
    else:
        carry = hc_ref[...]
        parts = []
        for gi in range(tm // SUBLANES):
            sl = slice(gi * SUBLANES, (gi + 1) * SUBLANES)
            hg = u_c[sl] + a_c[sl] * carry
            parts.append(hg)
            carry = jnp.broadcast_to(hg[SUBLANES - 1:SUBLANES], (SUBLANES, w))
        hs = jnp.concatenate(parts, axis=0)
        hc_ref[...] = carry
        prev_ref[...] = xr[tm - SUBLANES:]

    y = (hs * gate).astype(BF16)
    o_ref[...] = x + _dot(y, wout_ref[...])
    if sample:
        xr_ref[...] = xr
        hs_ref[...] = hs
    else:
        xr_ref[0] = xr[tm - SUBLANES:]
        hs_ref[0] = hs[tm - SUBLANES:]


def _lru(x, g, win, cw, cb, wax, bax, lam, wout, batch=None, seq=None, sample_state=None):
    t, d = x.shape
    w = wout.shape[0]
    bw = w // LRU_BLOCKS
    sample = sample_state is not None
    const2 = lambda *_: (0, 0)
    const3 = lambda *_: (0, 0, 0)
    single = pl.Buffered(1)
    if sample:
        tm = t
        grid = (1,)
        row = lambda i: (0, 0)
        dims = ("arbitrary",)
    else:
        tm = 256
        nt = seq // tm
        grid = (batch, nt)
        row = lambda b, i: (b * nt + i, 0)
        dims = ("arbitrary", "arbitrary")
    in_specs = [
        pl.BlockSpec((tm, d), row),
        pl.BlockSpec((1, d), const2),
        pl.BlockSpec((d, 2 * w), const2, pipeline_mode=single),
        pl.BlockSpec((CONV_WIDTH, w), const2),
        pl.BlockSpec((1, w), const2),
        pl.BlockSpec((LRU_BLOCKS, bw, 2 * bw), const3),
        pl.BlockSpec((LRU_BLOCKS, 1, 2 * bw), const3),
        pl.BlockSpec((1, w), const2),
        pl.BlockSpec((w, d), const2, pipeline_mode=single),
    ]
    args = [x, g.reshape(1, d), win, cw, cb.reshape(1, w), wax, bax, lam.reshape(1, w), wout]
    scratch = [pltpu.VMEM((tm, w), F32), pltpu.VMEM((tm, w), F32)]
    if sample:
        in_specs += [pl.BlockSpec((tm, w), row)] * 4
        args += list(sample_state)
        out_specs = [pl.BlockSpec((tm, d), row), pl.BlockSpec((tm, w), row), pl.BlockSpec((tm, w), row)]
        out_shape = [jax.ShapeDtypeStruct((t, d), F32), jax.ShapeDtypeStruct((t, w), F32),
                     jax.ShapeDtypeStruct((t, w), F32)]
    else:
        tail = lambda b, i: (b, 0, 0)
        out_specs = [pl.BlockSpec((tm, d), row), pl.BlockSpec((1, SUBLANES, w), tail),
                     pl.BlockSpec((1, SUBLANES, w), tail)]
        out_shape = [jax.ShapeDtypeStruct((t, d), F32), jax.ShapeDtypeStruct((batch, SUBLANES, w), F32),
                     jax.ShapeDtypeStruct((batch, SUBLANES, w), F32)]
        scratch += [pltpu.VMEM((SUBLANES, w), F32), pltpu.VMEM((SUBLANES, w), F32)]
    return pl.pallas_call(
        functools.partial(_lru_body, sample=sample, tm=tm, width=w),
        grid=grid,
        in_specs=in_specs,
        out_specs=out_specs,
        out_shape=out_shape,
        scratch_shapes=scratch,
        compiler_params=pltpu.CompilerParams(
            dimension_semantics=dims,
            vmem_limit_bytes=_vmem_limit(40 * 1024 * 1024)),
        name="lru_sample" if sample else "lru_prompt",
    )(*args)


def kernel(x_prompt, x_sample, cache_k, cache_v, cache_kidx, page_table, state_conv, state_h, rel_bias, attn_w_in, attn_w_out, lru_w_in, lru_conv_w, lru_conv_b, lru_w_a, lru_b_a, lru_w_x, lru_b_x, lru_lambda, lru_w_out, norm_g, ffn_w_gu, ffn_w_down, final_norm_g):
    batch, seq, d = x_prompt.shape
    n_dec, dec_seq, _ = x_sample.shape
    n_pages = page_table.shape[1]
    n_pool = cache_k.shape[0]
    a = ATTN_WIDTH
    w = lru_w_out.shape[0]
    assert seq % SCORE_CHUNK == 0 and dec_seq == SUBLANES and page_table.shape[1] >= 2

    xp = x_prompt.reshape(batch * seq, d)
    xs = x_sample.reshape(n_dec * dec_seq, d)

    wgu = ffn_w_gu.astype(BF16)
    wdn = ffn_w_down.astype(BF16)
    qi_end = 3 * a + IDX_HEADS * IDX_DIM
    wqkv = attn_w_in[:, :3 * a].astype(BF16)
    wqi = attn_w_in[:, 3 * a:qi_end].astype(BF16)
    wkw = jnp.pad(attn_w_in[:, qi_end:], ((0, 0), (0, LANES - (IDX_DIM + IDX_HEADS)))).astype(BF16)
    wo = attn_w_out.astype(BF16)
    lwin = lru_w_in.astype(BF16)
    lwout = lru_w_out.astype(BF16)
    wax = jnp.concatenate([lru_w_a, lru_w_x], axis=-1).astype(BF16)
    bax = jnp.concatenate([lru_b_a, lru_b_x], axis=-1)[:, None, :]

    xp = _ffn(xp, norm_g[0, 0], wgu[0, 0], wdn[0, 0])
    xs = _ffn(xs, norm_g[0, 0], wgu[0, 0], wdn[0, 0])

    q_p, k_p, v_p, kb_p, vb_p, qi_p, kw_p = _attn_proj(xp, norm_g[0, 1], wqkv, wqi, wkw)
    q_s, k_s, v_s, _, _, qi_s, kw_s = _attn_proj(xs, norm_g[0, 1], wqkv, wqi, wkw)

    nqb = seq // Q_BLOCK
    qi4 = qi_p.reshape(batch, nqb, Q_BLOCK, IDX_HEADS, IDX_DIM).transpose(0, 1, 3, 2, 4)
    qi4 = qi4.reshape(batch, nqb, IDX_HEADS * Q_BLOCK, IDX_DIM)
    kit = kw_p[:, :IDX_DIM].astype(BF16).reshape(batch, seq, IDX_DIM).transpose(0, 2, 1)
    kt = kb_p.reshape(batch, seq, a).transpose(0, 2, 1)
    vb3 = vb_p.reshape(batch, seq, a)
    i_q = np.arange(Q_BLOCK)[:, None]
    c_k = np.arange(2 * Q_BLOCK)[None, :]
    toe = _bias_rows(rel_bias, Q_BLOCK + i_q - c_k)
    toe = toe.reshape(N_PAIRS, 2 * Q_BLOCK, 2 * Q_BLOCK)
    xp = _attn_prompt(xp, q_p, kt, vb3, qi4, kw_p, kit, toe, wo, batch, seq)

    qi_s3 = qi_s.reshape(n_dec, dec_seq, IDX_HEADS, IDX_DIM).transpose(0, 2, 1, 3)
    qi_s3 = qi_s3.reshape(n_dec, IDX_HEADS * dec_seq, IDX_DIM)
    wi_s = kw_s[:, IDX_DIM:IDX_DIM + IDX_HEADS].reshape(n_dec, dec_seq, IDX_HEADS) * (IDX_HEADS ** -0.5)
    wb = jnp.broadcast_to(wi_s.transpose(0, 2, 1).reshape(n_dec, IDX_HEADS * dec_seq, 1),
                          (n_dec, IDX_HEADS * dec_seq, LANES))
    pad_rows = ((0, 0), (0, PAGE_SIZE - dec_seq), (0, 0))
    ki_new = jnp.pad(kw_s[:, :IDX_DIM].reshape(n_dec, dec_seq, IDX_DIM), pad_rows)
    scores = _sample_scores(page_table, qi_s3, wb, cache_kidx, ki_new, dec_seq)
    k_new = jnp.pad(k_s.reshape(n_dec, dec_seq, a), pad_rows)
    v_new = jnp.pad(v_s.reshape(n_dec, dec_seq, a), pad_rows)
    tok = np.arange(dec_seq)[:, None]
    jj = np.arange(PAGE_SIZE)[None, :]
    b_last = _bias_rows(rel_bias, PAGE_SIZE + tok - jj)
    b_new = _bias_rows(rel_bias, tok - jj)
    bias3 = jnp.stack([jnp.zeros_like(b_last), b_last, b_new]).reshape(3, N_HEADS * dec_seq, PAGE_SIZE)
    xs3 = _attn_sample(page_table, scores, q_s.astype(F32).reshape(n_dec, dec_seq, a),
                       cache_k.reshape(n_pool, PAGE_SIZE, a), cache_v.reshape(n_pool, PAGE_SIZE, a),
                       k_new, v_new, bias3, xs.reshape(n_dec, dec_seq, d), wo, dec_seq)
    xs = xs3.reshape(n_dec * dec_seq, d)

    xp = _ffn(xp, norm_g[0, 2], wgu[0, 1], wdn[0, 1])
    xs = _ffn(xs, norm_g[0, 2], wgu[0, 1], wdn[0, 1])

    xp = _ffn(xp, norm_g[1, 0], wgu[1, 0], wdn[1, 0])
    xs = _ffn(xs, norm_g[1, 0], wgu[1, 0], wdn[1, 0])

    lru_args = (lwin, lru_conv_w, lru_conv_b, wax, bax, lru_lambda, lwout)
    xp, conv_p8, h_p8 = _lru(xp, norm_g[1, 1], *lru_args, batch=batch, seq=seq)
    firsts = []
    for k in range(1, CONV_WIDTH):
        r = jnp.pad(state_conv[:, CONV_WIDTH - 1 - k:, :], ((0, 0), (0, dec_seq - k), (0, 0)))
        firsts.append(r.reshape(n_dec * dec_seq, w))
    h0 = jnp.broadcast_to(state_h[:, None, :], (n_dec, dec_seq, w)).reshape(n_dec * dec_seq, w)
    xs, xr_s, hs_s = _lru(xs, norm_g[1, 1], *lru_args, sample_state=firsts + [h0])

    xp = _ffn(xp, norm_g[1, 2], wgu[1, 1], wdn[1, 1], gf=final_norm_g)
    xs = _ffn(xs, norm_g[1, 2], wgu[1, 1], wdn[1, 1], gf=final_norm_g)

    nh, hd = N_HEADS, HEAD_DIM
    keep = CONV_WIDTH - 1
    return (
        xp.reshape(batch, seq, d),
        xs.reshape(n_dec, dec_seq, d),
        k_p.reshape(batch, seq, nh, hd),
        v_p.reshape(batch, seq, nh, hd),
        kw_p[:, :IDX_DIM].reshape(batch, seq, IDX_DIM),
        k_s.reshape(n_dec, dec_seq, nh, hd),
        v_s.reshape(n_dec, dec_seq, nh, hd),
        kw_s[:, :IDX_DIM].reshape(n_dec, dec_seq, IDX_DIM),
        conv_p8[:, SUBLANES - keep:, :],
        h_p8[:, SUBLANES - 1, :],
        xr_s.reshape(n_dec, dec_seq, w)[:, dec_seq - keep:, :],
        hs_s.reshape(n_dec, dec_seq, w)[:, dec_seq - 1, :],
    )
```

```python
import functools
import math

import numpy as np
import jax
import jax.numpy as jnp
from jax import lax
from jax.experimental import pallas as pl
from jax.experimental.pallas import tpu as pltpu

F32 = jnp.float32
BF16 = jnp.bfloat16
NEG_INF = float("-inf")

N_HEADS = 16
HEAD_DIM = 64
ATTN_WIDTH = N_HEADS * HEAD_DIM
IDX_HEADS = 8
IDX_DIM = 64
TOPK_MAX = 256
Q_BLOCK = 128
PAGE_SIZE = 128
N_BUCKETS = 32
MAX_DISTANCE = 128
LRU_BLOCKS = 8
CONV_WIDTH = 4
LRU_C = 8.0
RMS_EPS = 1e-6

V7X_VMEM_BYTES = 64 * 1024 * 1024
LANES = 128
SUBLANES = 8

N_PAIRS = N_HEADS // 2
SCORE_CHUNK = 512
WIDE_CHUNK = 512
SCORE_PAGES = 8
ATTN_PAGES = 4
TAIL_BLOCKS = 5
MAX_BISECT_ITERS = 320


def _vmem_limit(nbytes):
    return int(min(nbytes, V7X_VMEM_BYTES - 4 * 1024 * 1024))


def _rms(x, g):
    ms = jnp.mean(x * x, axis=-1, keepdims=True)
    return x * lax.rsqrt(ms + RMS_EPS) * g


def _sigmoid(x):
    return 1.0 / (1.0 + jnp.exp(-x))


def _dot(a, b):
    return jnp.dot(a, b, preferred_element_type=F32)


def _dot_nt(a, b):
    return lax.dot_general(a, b, (((1,), (1,)), ((), ())), preferred_element_type=F32)


def _fold_lanes(x, op):
    out = x[:, :LANES]
    for j in range(1, x.shape[1] // LANES):
        out = op(out, x[:, j * LANES:(j + 1) * LANES])
    return out


def _ffn_body(*refs, d_ff, fc, final):
    if final:
        x_ref, g_ref, wgu_ref, wd_ref, gf_ref, o_ref = refs
    else:
        x_ref, g_ref, wgu_ref, wd_ref, o_ref = refs
    x = x_ref[...]
    h = _rms(x, g_ref[...]).astype(BF16)
    acc = jnp.zeros(x.shape, F32)
    for c in range(d_ff // fc):
        gg = _dot(h, wgu_ref[:, c * fc:(c + 1) * fc])
        uu = _dot(h, wgu_ref[:, d_ff + c * fc:d_ff + (c + 1) * fc])
        a = (gg * _sigmoid(gg) * uu).astype(BF16)
        acc = acc + _dot(a, wd_ref[c * fc:(c + 1) * fc, :])
    y = x + 0.5 * acc
    if final:
        y = _rms(y, gf_ref[...])
    o_ref[...] = y


def _ffn(x, g, wgu, wd, gf=None):
    t, d = x.shape
    d_ff = wd.shape[0]
    tm = 512 if t % 512 == 0 else 256
    fc = 256
    final = gf is not None
    const = lambda i: (0, 0)
    in_specs = [
        pl.BlockSpec((tm, d), lambda i: (i, 0)),
        pl.BlockSpec((1, d), const),
        pl.BlockSpec((d, 2 * d_ff), const, pipeline_mode=pl.Buffered(1)),
        pl.BlockSpec((d_ff, d), const, pipeline_mode=pl.Buffered(1)),
    ]
    args = [x, g.reshape(1, d), wgu, wd]
    if final:
        in_specs.append(pl.BlockSpec((1, d), const))
        args.append(gf.reshape(1, d))
    return pl.pallas_call(
        functools.partial(_ffn_body, d_ff=d_ff, fc=fc, final=final),
        grid=(t // tm,),
        in_specs=in_specs,
        out_specs=pl.BlockSpec((tm, d), lambda i: (i, 0)),
        out_shape=jax.ShapeDtypeStruct((t, d), F32),
        compiler_params=pltpu.CompilerParams(
            dimension_semantics=("arbitrary",),
            vmem_limit_bytes=_vmem_limit(48 * 1024 * 1024)),
        name="ffn_final" if final else "ffn",
    )(*args)


def _proj_body(x_ref, g_ref, wqkv_ref, wqi_ref, wkw_ref,
               q_ref, k_ref, v_ref, kb_ref, vb_ref, qi_ref, kw_ref):
    h = _rms(x_ref[...], g_ref[...]).astype(BF16)
    a = ATTN_WIDTH
    q_ref[...] = (_dot(h, wqkv_ref[:, :a]) * (HEAD_DIM ** -0.5)).astype(BF16)
    k = _dot(h, wqkv_ref[:, a:2 * a])
    k_ref[...] = k
    kb_ref[...] = k.astype(BF16)
    v = _dot(h, wqkv_ref[:, 2 * a:])
    v_ref[...] = v
    vb_ref[...] = v.astype(BF16)
    qi_ref[...] = (_dot(h, wqi_ref[...]) * (IDX_DIM ** -0.5)).astype(BF16)
    kw_ref[...] = _dot(h, wkw_ref[...])


def _attn_proj(x, g, wqkv, wqi, wkw):
    t, d = x.shape
    tm = 256
    a = ATTN_WIDTH
    nqi = IDX_HEADS * IDX_DIM
    const = lambda i: (0, 0)
    row = lambda i: (i, 0)
    return pl.pallas_call(
        _proj_body,
        grid=(t // tm,),
        in_specs=[
            pl.BlockSpec((tm, d), row),
            pl.BlockSpec((1, d), const),
            pl.BlockSpec((d, 3 * a), const, pipeline_mode=pl.Buffered(1)),
            pl.BlockSpec((d, nqi), const, pipeline_mode=pl.Buffered(1)),
            pl.BlockSpec((d, LANES), const, pipeline_mode=pl.Buffered(1)),
        ],
        out_specs=[
            pl.BlockSpec((tm, a), row), pl.BlockSpec((tm, a), row), pl.BlockSpec((tm, a), row),
            pl.BlockSpec((tm, a), row), pl.BlockSpec((tm, a), row),
            pl.BlockSpec((tm, nqi), row), pl.BlockSpec((tm, LANES), row),
        ],
        out_shape=[
            jax.ShapeDtypeStruct((t, a), BF16), jax.ShapeDtypeStruct((t, a), F32),
            jax.ShapeDtypeStruct((t, a), F32), jax.ShapeDtypeStruct((t, a), BF16),
            jax.ShapeDtypeStruct((t, a), BF16), jax.ShapeDtypeStruct((t, nqi), BF16),
            jax.ShapeDtypeStruct((t, LANES), F32),
        ],
        compiler_params=pltpu.CompilerParams(
            dimension_semantics=("arbitrary",),
            vmem_limit_bytes=_vmem_limit(40 * 1024 * 1024)),
        name="attn_proj",
    )(x, g.reshape(1, d), wqkv, wqi, wkw)


def _select_topk(count, smin, smax, n_adm, n_sel, n_cols):
    ksel = float(n_sel)
    c_hi = count(lambda s, col: s >= smax)
    few = n_adm <= ksel
    top_tied = jnp.logical_and(jnp.logical_not(few), c_hi >= ksel)
    lo0 = jnp.where(top_tied, smax, smin)
    done0 = jnp.where(jnp.logical_or(few, top_tied), 1.0, 0.0)

    def not_all_done(done):
        return (jnp.min(done) < 0.5).astype(jnp.int32)

    def cond(carry):
        return jnp.logical_and(carry[3] > 0, carry[4] < MAX_BISECT_ITERS)

    def body(carry):
        lo, hi, done, _, it = carry
        mid = lo * 0.5 + hi * 0.5
        conv = jnp.logical_or(mid <= lo, mid >= hi)
        c = count(lambda s, col: s >= mid)
        ge = c >= ksel
        upd = jnp.logical_and(done < 0.5, jnp.logical_not(conv))
        lo = jnp.where(jnp.logical_and(upd, ge), mid, lo)
        hi = jnp.where(jnp.logical_and(upd, jnp.logical_not(ge)), mid, hi)
        done = jnp.where(jnp.logical_or(conv, c == ksel), 1.0, done)
        return lo, hi, done, not_all_done(done), it + 1

    thr = lax.while_loop(cond, body, (lo0, smax, done0, not_all_done(done0), jnp.int32(0)))[0]

    c_ge = count(lambda s, col: s >= thr)
    has_tie = c_ge > ksel
    big = float(n_cols)

    def tie_break(_):
        c_gt = count(lambda s, col: s > thr)
        need = ksel - c_gt
        jlo = jnp.full_like(thr, -1.0)
        jhi = jnp.full_like(thr, big)

        def jbody(_, jc):
            jlo, jhi = jc
            jm = jnp.floor((jlo + jhi) * 0.5)
            c = count(lambda s, col: jnp.logical_and(s == thr, col <= jm))
            ok = c >= need
            return jnp.where(ok, jlo, jm), jnp.where(ok, jm, jhi)

        n_it = int(math.ceil(math.log2(n_cols + 2))) + 1
        _, jhi = lax.fori_loop(0, n_it, jbody, (jlo, jhi))
        return jnp.where(has_tie, jhi, big)

    any_tie = jnp.max(jnp.where(has_tie, 1.0, 0.0)) > 0.5
    jmax = lax.cond(any_tie, tie_break, lambda _: jnp.full_like(thr, big), 0)
    return thr, jmax


def _t5_bucket_np(rel):
    n = np.maximum(rel, 0)
    max_exact = N_BUCKETS // 2
    nf = np.maximum(n, max_exact).astype(np.float32)
    large = max_exact + (np.log(nf / np.float32(max_exact)) / np.float32(math.log(MAX_DISTANCE / max_exact))
                         * (N_BUCKETS - max_exact)).astype(np.int32)
    large = np.minimum(large, N_BUCKETS - 1)
    return np.where(n < max_exact, n, large)


_FAR_BUCKET = int(_t5_bucket_np(np.array([1 << 20]))[0])
assert int(_t5_bucket_np(np.array([Q_BLOCK + 1]))[0]) == _FAR_BUCKET


def _bias_rows(rel_bias, rel):
    bucket = _t5_bucket_np(rel)
    b = jnp.take(rel_bias, jnp.asarray(bucket.reshape(-1)), axis=0).reshape(rel.shape + (N_HEADS,))
    b = b - rel_bias[_FAR_BUCKET][None, None, :]
    return jnp.transpose(b, (2, 0, 1))


def _attn_prompt_body(qi_ref, kw_ref, kit_ref, q_ref, kt_ref, v_ref, toe_ref, x_ref, wo_ref,
                      o_ref, s_ref, lg_ref, oacc_ref, *, seq, n_sel):
    qb = pl.program_id(1)
    qn = Q_BLOCK
    sc = SCORE_CHUNK
    n_sc = (qb + sc // qn) // (sc // qn)

    qi = qi_ref[0, 0]
    wi = kw_ref[:, IDX_DIM:IDX_DIM + IDX_HEADS] * (IDX_HEADS ** -0.5)
    wcols = [jnp.broadcast_to(wi[:, h:h + 1], (qn, sc)) for h in range(IDX_HEADS)]
    qpos = lax.broadcasted_iota(jnp.int32, (qn, sc), 0) + qb * qn
    lane = lax.broadcasted_iota(jnp.int32, (qn, sc), 1)

    def score_chunk(c, carry):
        smin, smax = carry
        c0 = pl.multiple_of(c * sc, sc)
        s = _dot(qi, kit_ref[0, :, pl.ds(c0, sc)])
        tot = jnp.maximum(s[:qn], 0.0) * wcols[0]
        for h in range(1, IDX_HEADS):
            tot = tot + jnp.maximum(s[h * qn:(h + 1) * qn], 0.0) * wcols[h]
        adm = (lane + c0) <= qpos
        s_ref[:, pl.ds(c0, sc)] = jnp.where(adm, tot, NEG_INF)
        smin = jnp.minimum(smin, _fold_lanes(jnp.where(adm, tot, jnp.inf), jnp.minimum))
        smax = jnp.maximum(smax, _fold_lanes(jnp.where(adm, tot, NEG_INF), jnp.maximum))
        return smin, smax

    smin, smax = lax.fori_loop(
        0, n_sc, score_chunk,
        (jnp.full((qn, LANES), jnp.inf, F32), jnp.full((qn, LANES), NEG_INF, F32)))
    smin = jnp.min(smin, axis=1, keepdims=True)
    smax = jnp.max(smax, axis=1, keepdims=True)

    def count(pred):
        def body(c, acc):
            c0 = pl.multiple_of(c * sc, sc)
            s = s_ref[:, pl.ds(c0, sc)]
            col = (lane + c0).astype(F32)
            return acc + _fold_lanes(jnp.where(pred(s, col), 1.0, 0.0), jnp.add)
        acc = lax.fori_loop(0, n_sc, body, jnp.zeros((qn, LANES), F32))
        return jnp.sum(acc, axis=1, keepdims=True)

    n_adm = (lax.broadcasted_iota(jnp.int32, (qn, 1), 0) + qb * qn + 1).astype(F32)
    thr, jmax = _select_topk(count, smin, smax, n_adm, n_sel, seq)

    def mask_chunk(c, _):
        c0 = pl.multiple_of(c * sc, sc)
        s = s_ref[:, pl.ds(c0, sc)]
        col = (lane + c0).astype(F32)
        sel = jnp.logical_or(s > thr, jnp.logical_and(s == thr, col <= jmax))
        sel = jnp.logical_and(sel, s > NEG_INF)
        s_ref[:, pl.ds(c0, sc)] = jnp.where(sel, 0.0, NEG_INF)
        return 0

    lax.fori_loop(0, n_sc, mask_chunk, 0)

    n_far = jnp.maximum(qb - 1, 0)
    wc = WIDE_CHUNK
    n_wide = n_far // (wc // qn)
    tail0 = pl.multiple_of(n_wide * wc, wc)
    tw = TAIL_BLOCKS * qn
    near0 = pl.multiple_of(tail0 + jnp.where(qb > 0, n_far % (wc // qn), 0) * qn, qn)
    toe0 = pl.multiple_of(jnp.where(qb > 0, 0, qn), qn)
    s_ref[:, pl.ds(pl.multiple_of(n_sc * sc, sc), qn)] = jnp.full((qn, qn), NEG_INF, F32)
    lane_p = lax.broadcasted_iota(jnp.int32, (qn, LANES), 1)

    def pair_body(pr, _):
        p0 = pl.multiple_of(pr * LANES, LANES)
        qp = q_ref[:, pl.ds(p0, LANES)]
        zero = jnp.zeros_like(qp)
        qs = jnp.concatenate([jnp.where(lane_p < HEAD_DIM, qp, zero),
                              jnp.where(lane_p >= HEAD_DIM, qp, zero)], axis=0)

        def logits(c0, width):
            lg = _dot(qs, kt_ref[0, pl.ds(p0, LANES), pl.ds(c0, width)])
            m = s_ref[:, pl.ds(c0, width)]
            return lg + jnp.concatenate([m, m], axis=0)

        def p1(c0, width, mrun):
            lg = logits(c0, width)
            lg_ref[:, pl.ds(c0, width)] = lg
            return jnp.maximum(mrun, _fold_lanes(lg, jnp.maximum))

        def p2(c0, width, carry):
            lrun, acc = carry
            p = jnp.exp(lg_ref[:, pl.ds(c0, width)] - m)
            lrun = lrun + _fold_lanes(p, jnp.add)
            acc = acc + _dot(p.astype(BF16), v_ref[0, pl.ds(c0, width), pl.ds(p0, LANES)])
            return lrun, acc

        def over_wide(step, carry):
            done = 0
            for mult in (4, 2, 1):
                width = mult * wc
                trips = n_wide // 4 if mult == 4 else (n_wide // mult) % 2

                def body(i, c, width=width, base=done):
                    return step(pl.multiple_of(base + i * width, wc), width, c)

                carry = lax.fori_loop(0, trips, body, carry)
                done = done + trips * width
            return carry

        mrun = over_wide(p1, jnp.full((2 * qn, LANES), NEG_INF, F32))
        lg_ref[:, pl.ds(tail0, tw)] = logits(tail0, tw)
        lg_ref[:, pl.ds(near0, 2 * qn)] = lg_ref[:, pl.ds(near0, 2 * qn)] + toe_ref[pr, :, pl.ds(toe0, 2 * qn)]
        mrun = jnp.maximum(mrun, _fold_lanes(lg_ref[:, pl.ds(tail0, tw)], jnp.maximum))
        m = jnp.max(mrun, axis=1, keepdims=True)

        carry = over_wide(p2, (jnp.zeros((2 * qn, LANES), F32), jnp.zeros((2 * qn, LANES), F32)))
        lrun, acc = p2(tail0, tw, carry)
        o = acc / jnp.sum(lrun, axis=1, keepdims=True)
        oacc_ref[:, pl.ds(p0, LANES)] = jnp.where(lane_p < HEAD_DIM, o[:qn], o[qn:])
        return 0

    lax.fori_loop(0, N_PAIRS, pair_body, 0)

    o_ref[...] = x_ref[...] + _dot(oacc_ref[...].astype(BF16), wo_ref[...])


def _attn_prompt(x, q, kt, vb, qi4, kw, kit, toe, wo, batch, seq):
    d = x.shape[1]
    a = ATTN_WIDTH
    nqb = seq // Q_BLOCK
    n_sel = min(TOPK_MAX, seq // 4)
    blk = lambda b, i: (b * nqb + i, 0)
    per_b3 = lambda b, i: (b, 0, 0)
    seqp = kt.shape[2]
    assert seqp >= seq + (TAIL_BLOCKS - WIDE_CHUNK // Q_BLOCK) * Q_BLOCK
    resident = 2 * seqp * a * 2 + Q_BLOCK * seqp * 4 + 2 * Q_BLOCK * seqp * 4
    return pl.pallas_call(
        functools.partial(_attn_prompt_body, seq=seq, n_sel=n_sel),
        grid=(batch, nqb),
        in_specs=[
            pl.BlockSpec((1, 1, IDX_HEADS * Q_BLOCK, IDX_DIM), lambda b, i: (b, i, 0, 0)),
            pl.BlockSpec((Q_BLOCK, LANES), blk),
            pl.BlockSpec((1, IDX_DIM, seq), per_b3),
            pl.BlockSpec((Q_BLOCK, a), blk),
            pl.BlockSpec((1, a, seqp), per_b3, pipeline_mode=pl.Buffered(1)),
            pl.BlockSpec((1, seqp, a), per_b3, pipeline_mode=pl.Buffered(1)),
            pl.BlockSpec((N_PAIRS, 2 * Q_BLOCK, 3 * Q_BLOCK), lambda b, i: (0, 0, 0), pipeline_mode=pl.Buffered(1)),
            pl.BlockSpec((Q_BLOCK, d), blk),
            pl.BlockSpec((a, d), lambda b, i: (0, 0), pipeline_mode=pl.Buffered(1)),
        ],
        out_specs=pl.BlockSpec((Q_BLOCK, d), blk),
        out_shape=jax.ShapeDtypeStruct(x.shape, F32),
        scratch_shapes=[
            pltpu.VMEM((Q_BLOCK, seqp), F32),
            pltpu.VMEM((2 * Q_BLOCK, seqp), F32),
            pltpu.VMEM((Q_BLOCK, a), F32),
        ],
        compiler_params=pltpu.CompilerParams(
            dimension_semantics=("arbitrary", "arbitrary"),
            vmem_limit_bytes=_vmem_limit(resident + 14 * 1024 * 1024)),
        name="attn_prompt",
    )(qi4, kw, kit, q, kt, vb, toe, x, wo)


def _sample_scores_body(pt_ref, qi_ref, wb_ref, *refs, n_grp, dec_seq):
    page_refs = refs[:n_grp]
    kinew_ref, o_ref, onew_ref = refs[n_grp:]
    qi = qi_ref[0]
    wb = wb_ref[0]

    def score(keys, n_tiles):
        s = _dot_nt(qi, keys.astype(BF16))
        t = jnp.maximum(s, 0.0) * jnp.concatenate([wb] * n_tiles, axis=1)
        tot = t[:dec_seq]
        for h in range(1, IDX_HEADS):
            tot = tot + t[h * dec_seq:(h + 1) * dec_seq]
        return tot

    o_ref[0] = score(jnp.concatenate([r[0] for r in page_refs], axis=0), n_grp)

    @pl.when(pl.program_id(1) == 0)
    def _():
        tot = score(kinew_ref[0], 1)
        j = lax.broadcasted_iota(jnp.int32, tot.shape, 1)
        tok = lax.broadcasted_iota(jnp.int32, tot.shape, 0)
        onew_ref[0] = jnp.where(j <= tok, tot, NEG_INF)


def _sample_scores(page_table, qi_s, wb, cache_kidx, ki_new_pad, dec_seq):
    n, n_pages = page_table.shape
    n_grp = SCORE_PAGES if n_pages % SCORE_PAGES == 0 else 1
    rows = IDX_HEADS * dec_seq
    per_seq = lambda i, g, pt: (i, 0, 0)

    def page(j):
        return lambda i, g, pt: (pt[i, g * n_grp + j], 0, 0)

    grid_spec = pltpu.PrefetchScalarGridSpec(
        num_scalar_prefetch=1,
        grid=(n, n_pages // n_grp),
        in_specs=[pl.BlockSpec((1, rows, IDX_DIM), per_seq), pl.BlockSpec((1, rows, LANES), per_seq)]
        + [pl.BlockSpec((1, PAGE_SIZE, IDX_DIM), page(j)) for j in range(n_grp)]
        + [pl.BlockSpec((1, PAGE_SIZE, IDX_DIM), per_seq)],
        out_specs=[pl.BlockSpec((1, dec_seq, n_grp * PAGE_SIZE), lambda i, g, pt: (i, 0, g)),
                   pl.BlockSpec((1, dec_seq, PAGE_SIZE), per_seq)],
    )
    return pl.pallas_call(
        functools.partial(_sample_scores_body, n_grp=n_grp, dec_seq=dec_seq),
        grid_spec=grid_spec,
        out_shape=[jax.ShapeDtypeStruct((n, dec_seq, n_pages * PAGE_SIZE), F32),
                   jax.ShapeDtypeStruct((n, dec_seq, PAGE_SIZE), F32)],
        compiler_params=pltpu.CompilerParams(dimension_semantics=("arbitrary", "arbitrary")),
        name="sample_scores",
    )(page_table, qi_s, wb, *([cache_kidx] * n_grp), ki_new_pad)


def _attn_sample_body(pt_ref, scp_ref, scn_ref, q_ref, *refs, n_pages, n_grp, dec_seq, n_sel):
    k_refs = refs[:n_grp]
    v_refs = refs[n_grp:2 * n_grp]
    (knew_ref, vnew_ref, bias_ref, x_ref, wo_ref,
     o_ref, mask_ref, q16_ref, m_ref, l_ref, acc_ref) = refs[2 * n_grp:]
    g = pl.program_id(1)
    n_groups = n_pages // n_grp
    n_cols = (n_pages + 1) * PAGE_SIZE
    ds = dec_seq

    @pl.when(g == 0)
    def _():
        s = jnp.concatenate([scp_ref[0], scn_ref[0]], axis=1)
        col = lax.broadcasted_iota(jnp.int32, s.shape, 1).astype(F32)
        fin = s > NEG_INF
        smin = jnp.min(jnp.where(fin, s, jnp.inf), axis=1, keepdims=True)
        smax = jnp.max(s, axis=1, keepdims=True)

        def count(pred):
            return jnp.sum(jnp.where(pred(s, col), 1.0, 0.0), axis=1, keepdims=True)

        tok = lax.broadcasted_iota(jnp.int32, (ds, 1), 0)
        n_adm = (tok + n_pages * PAGE_SIZE + 1).astype(F32)
        thr, jmax = _select_topk(count, smin, smax, n_adm, n_sel, n_cols)
        sel = jnp.logical_or(s > thr, jnp.logical_and(s == thr, col <= jmax))
        sel = jnp.logical_and(sel, fin)
        mask_ref[...] = jnp.where(sel, 0.0, NEG_INF)
        q = q_ref[0]
        for h in range(N_HEADS):
            qh = q[:, h * HEAD_DIM:(h + 1) * HEAD_DIM]
            q16_ref[h] = jnp.concatenate([qh, jnp.zeros_like(qh)], axis=0).astype(BF16)
        m_ref[...] = jnp.full(m_ref.shape, NEG_INF, F32)
        l_ref[...] = jnp.zeros(l_ref.shape, F32)
        acc_ref[...] = jnp.zeros(acc_ref.shape, F32)

    def head_major(page_refs):
        pages = [pltpu.einshape("khd->hkd", r[0]) for r in page_refs]
        return [jnp.concatenate([p[h] for p in pages], axis=0).astype(BF16) for h in range(N_HEADS)]

    def pad_rows16(x):
        return jnp.concatenate([x, jnp.zeros_like(x)], axis=0).astype(BF16)

    def attend(k_refs, v_refs, mk, bias):
        kh = head_major(k_refs)
        vh = head_major(v_refs)
        lg = jnp.concatenate([_dot_nt(q16_ref[h], kh[h])[:ds] for h in range(N_HEADS)], axis=0)
        lg = lg + jnp.concatenate([mk] * N_HEADS, axis=0) + bias
        m_old = m_ref[...]
        m_new = jnp.maximum(m_old, jnp.max(lg, axis=1, keepdims=True))
        m_safe = jnp.where(m_new == NEG_INF, 0.0, m_new)
        alpha = jnp.exp(m_old - m_safe)
        pr = jnp.exp(lg - m_safe)
        l_ref[...] = alpha * l_ref[...] + jnp.sum(pr, axis=1, keepdims=True)
        pv = jnp.concatenate([_dot(pad_rows16(pr[h * ds:(h + 1) * ds]), vh[h])[:ds]
                              for h in range(N_HEADS)], axis=0)
        acc_ref[...] = alpha * acc_ref[...] + pv
        m_ref[...] = m_new

    @pl.when(g < n_groups)
    def _():
        c0 = pl.multiple_of(g * (n_grp * PAGE_SIZE), n_grp * PAGE_SIZE)
        mk = mask_ref[:, pl.ds(c0, n_grp * PAGE_SIZE)]
        near = jnp.where(g == n_groups - 1, bias_ref[1], 0.0)
        if n_grp > 1:
            near = jnp.concatenate([jnp.zeros((near.shape[0], (n_grp - 1) * PAGE_SIZE), F32), near], axis=1)
        attend(k_refs, v_refs, mk, near)

    @pl.when(g == n_groups)
    def _():
        attend([knew_ref], [vnew_ref], mask_ref[:, pl.ds(n_pages * PAGE_SIZE, PAGE_SIZE)], bias_ref[2])
        o = acc_ref[...] / l_ref[...]
        out = jnp.zeros((2 * ds, wo_ref.shape[1]), F32)
        for h in range(N_HEADS):
            out = out + _dot(pad_rows16(o[h * ds:(h + 1) * ds]), wo_ref[h * HEAD_DIM:(h + 1) * HEAD_DIM, :])
        o_ref[0] = x_ref[0] + out[:ds]


def _attn_sample(page_table, scores_past, scores_new, q_s, cache_k, cache_v, k_new, v_new, bias3, x_s, wo, dec_seq):
    n, n_pages = page_table.shape
    n_grp = ATTN_PAGES if n_pages % ATTN_PAGES == 0 else 1
    n_groups = n_pages // n_grp
    a = ATTN_WIDTH
    d = x_s.shape[-1]
    rows = N_HEADS * dec_seq
    page_shape = (1, PAGE_SIZE, N_HEADS, HEAD_DIM)
    n_cols = (n_pages + 1) * PAGE_SIZE
    n_sel = min(TOPK_MAX, (n_pages * PAGE_SIZE + dec_seq) // 4)
    per_seq = lambda i, g, pt: (i, 0, 0)
    per_seq4 = lambda i, g, pt: (i, 0, 0, 0)

    def page(j):
        return lambda i, g, pt: (pt[i, jnp.minimum(g, n_groups - 1) * n_grp + j], 0, 0, 0)

    page_specs = [pl.BlockSpec(page_shape, page(j)) for j in range(n_grp)]
    grid_spec = pltpu.PrefetchScalarGridSpec(
        num_scalar_prefetch=1,
        grid=(n, n_groups + 1),
        in_specs=[
            pl.BlockSpec((1, dec_seq, n_pages * PAGE_SIZE), per_seq),
            pl.BlockSpec((1, dec_seq, PAGE_SIZE), per_seq),
            pl.BlockSpec((1, dec_seq, a), per_seq),
        ] + page_specs + page_specs + [
            pl.BlockSpec(page_shape, per_seq4),
            pl.BlockSpec(page_shape, per_seq4),
            pl.BlockSpec((3, rows, PAGE_SIZE), lambda i, g, pt: (0, 0, 0)),
            pl.BlockSpec((1, dec_seq, d), per_seq),
            pl.BlockSpec((a, d), lambda i, g, pt: (0, 0)),
        ],
        out_specs=pl.BlockSpec((1, dec_seq, d), per_seq),
        scratch_shapes=[
            pltpu.VMEM((dec_seq, n_cols), F32),
            pltpu.VMEM((N_HEADS, 2 * dec_seq, HEAD_DIM), BF16),
            pltpu.VMEM((rows, 1), F32),
            pltpu.VMEM((rows, 1), F32),
            pltpu.VMEM((rows, HEAD_DIM), F32),
        ],
    )
    return pl.pallas_call(
        functools.partial(_attn_sample_body, n_pages=n_pages, n_grp=n_grp, dec_seq=dec_seq, n_sel=n_sel),
        grid_spec=grid_spec,
        out_shape=jax.ShapeDtypeStruct(x_s.shape, F32),
        compiler_params=pltpu.CompilerParams(
            dimension_semantics=("arbitrary", "arbitrary"),
            vmem_limit_bytes=_vmem_limit(44 * 1024 * 1024)),
        name="attn_sample",
    )(page_table, scores_past, scores_new, q_s, *([cache_k] * n_grp), *([cache_v] * n_grp),
      k_new, v_new, bias3, x_s, wo)


def _gelu_tanh(x):
    return 0.5 * x * (1.0 + jnp.tanh(math.sqrt(2.0 / math.pi) * (x + 0.044715 * (x * x * x))))


def _lru_body(*refs, sample, tm, width):
    if sample:
        (x_ref, g_ref, win_ref, cw_ref, cb_ref, wax_ref, bax_ref, lam_ref, wout_ref,
         r1_ref, r2_ref, r3_ref, h0_ref, o_ref, xr_ref, hs_ref, a_ref, u_ref) = refs
    else:
        (x_ref, g_ref, win_ref, cw_ref, cb_ref, wax_ref, bax_ref, lam_ref, wout_ref,
         o_ref, xr_ref, hs_ref, a_ref, u_ref, prev_ref, hc_ref) = refs

        @pl.when(pl.program_id(1) == 0)
        def _():
            prev_ref[...] = jnp.zeros(prev_ref.shape, F32)
            hc_ref[...] = jnp.zeros(hc_ref.shape, F32)

    w = width
    x = x_ref[...]
    h = _rms(x, g_ref[...]).astype(BF16)
    z = _dot(h, win_ref[...])
    gate = _gelu_tanh(z[:, :w])
    xr = z[:, w:]
    row = lax.broadcasted_iota(jnp.int32, (tm, w), 0)
    pos8 = row & (SUBLANES - 1)

    shifted = []
    for k in range(1, CONV_WIDTH):
        rolled = pltpu.roll(xr, k, 0)
        if sample:
            first = (r1_ref, r2_ref, r3_ref)[k - 1][...]
            shifted.append(jnp.where(pos8 < k, first, rolled))
        else:
            first = jnp.concatenate([pltpu.roll(prev_ref[...], k, 0)] * (tm // SUBLANES), axis=0)
            shifted.append(jnp.where(row < k, first, rolled))
    xc = cb_ref[...]
    for j in range(CONV_WIDTH - 1):
        xc = xc + shifted[CONV_WIDTH - 2 - j] * cw_ref[j:j + 1, :]
    xc = xc + xr * cw_ref[CONV_WIDTH - 1:CONV_WIDTH, :]

    lam = lam_ref[...]
    nl = -lam
    softplus = jnp.maximum(nl, 0.0) + jnp.log1p(jnp.exp(-jnp.abs(nl)))
    xcb = xc.astype(BF16)
    bw = w // LRU_BLOCKS
    for gi in range(LRU_BLOCKS):
        sl = slice(gi * bw, (gi + 1) * bw)
        ga = _dot(xcb[:, sl], wax_ref[gi]) + bax_ref[gi]
        r = _sigmoid(ga[:, :bw])
        ig = _sigmoid(ga[:, bw:])
        log_a = -LRU_C * r * softplus[:, sl]
        a_ref[:, sl] = jnp.exp(log_a)
        th = jnp.tanh(log_a)
        u_ref[:, sl] = jnp.sqrt(-2.0 * th / (1.0 - th)) * (ig * xc[:, sl])

    a_c = a_ref[...]
    u_c = u_ref[...]
    for s in (1, 2, 4):
        ok = pos8 >= s
        a_sh = jnp.where(ok, pltpu.roll(a_c, s, 0), 1.0)
        u_sh = jnp.where(ok, pltpu.roll(u_c, s, 0), 0.0)
        u_c = u_c + a_c * u_sh
        a_c = a_c * a_sh
    if sample:
        hs = u_c + a_c * h0_ref[...]
    else:
        carry = hc_ref[...]
        parts = []
        for gi in range(tm // SUBLANES):
            sl = slice(gi * SUBLANES, (gi + 1) * SUBLANES)
            hg = u_c[sl] + a_c[sl] * carry
            parts.append(hg)
            carry = jnp.broadcast_to(hg[SUBLANES - 1:SUBLANES], (SUBLANES, w))
        hs = jnp.concatenate(parts, axis=0)
        hc_ref[...] = carry
        prev_ref[...] = xr[tm - SUBLANES:]

    y = (hs * gate).astype(BF16)
    o_ref[...] = x + _dot(y, wout_ref[...])
    if sample:
        xr_ref[...] = xr
        hs_ref[...] = hs
    else:
        xr_ref[0] = xr[tm - SUBLANES:]
        hs_ref[0] = hs[tm - SUBLANES:]


def _lru(x, g, win, cw, cb, wax, bax, lam, wout, batch=None, seq=None, sample_state=None):
    t, d = x.shape
    w = wout.shape[0]
    bw = w // LRU_BLOCKS
    sample = sample_state is not None
    const2 = lambda *_: (0, 0)
    const3 = lambda *_: (0, 0, 0)
    single = pl.Buffered(1)
    if sample:
        tm = t
        grid = (1,)
        row = lambda i: (0, 0)
        dims = ("arbitrary",)
    else:
        tm = 256
        nt = seq // tm
        grid = (batch, nt)
        row = lambda b, i: (b * nt + i, 0)
        dims = ("arbitrary", "arbitrary")
    in_specs = [
        pl.BlockSpec((tm, d), row),
        pl.BlockSpec((1, d), const2),
        pl.BlockSpec((d, 2 * w), const2, pipeline_mode=single),
        pl.BlockSpec((CONV_WIDTH, w), const2),
        pl.BlockSpec((1, w), const2),
        pl.BlockSpec((LRU_BLOCKS, bw, 2 * bw), const3),
        pl.BlockSpec((LRU_BLOCKS, 1, 2 * bw), const3),
        pl.BlockSpec((1, w), const2),
        pl.BlockSpec((w, d), const2, pipeline_mode=single),
    ]
    args = [x, g.reshape(1, d), win, cw, cb.reshape(1, w), wax, bax, lam.reshape(1, w), wout]
    scratch = [pltpu.VMEM((tm, w), F32), pltpu.VMEM((tm, w), F32)]
    if sample:
        in_specs += [pl.BlockSpec((tm, w), row)] * 4
        args += list(sample_state)
        out_specs = [pl.BlockSpec((tm, d), row), pl.BlockSpec((tm, w), row), pl.BlockSpec((tm, w), row)]
        out_shape = [jax.ShapeDtypeStruct((t, d), F32), jax.ShapeDtypeStruct((t, w), F32),
                     jax.ShapeDtypeStruct((t, w), F32)]
    else:
        tail = lambda b, i: (b, 0, 0)
        out_specs = [pl.BlockSpec((tm, d), row), pl.BlockSpec((1, SUBLANES, w), tail),
                     pl.BlockSpec((1, SUBLANES, w), tail)]
        out_shape = [jax.ShapeDtypeStruct((t, d), F32), jax.ShapeDtypeStruct((batch, SUBLANES, w), F32),
                     jax.ShapeDtypeStruct((batch, SUBLANES, w), F32)]
        scratch += [pltpu.VMEM((SUBLANES, w), F32), pltpu.VMEM((SUBLANES, w), F32)]
    return pl.pallas_call(
        functools.partial(_lru_body, sample=sample, tm=tm, width=w),
        grid=grid,
        in_specs=in_specs,
        out_specs=out_specs,
        out_shape=out_shape,
        scratch_shapes=scratch,
        compiler_params=pltpu.CompilerParams(
            dimension_semantics=dims,
            vmem_limit_bytes=_vmem_limit(40 * 1024 * 1024)),
        name="lru_sample" if sample else "lru_prompt",
    )(*args)


def kernel(x_prompt, x_sample, cache_k, cache_v, cache_kidx, page_table, state_conv, state_h, rel_bias, attn_w_in, attn_w_out, lru_w_in, lru_conv_w, lru_conv_b, lru_w_a, lru_b_a, lru_w_x, lru_b_x, lru_lambda, lru_w_out, norm_g, ffn_w_gu, ffn_w_down, final_norm_g):
    batch, seq, d = x_prompt.shape
    n_dec, dec_seq, _ = x_sample.shape
    n_pages = page_table.shape[1]
    n_pool = cache_k.shape[0]
    a = ATTN_WIDTH
    w = lru_w_out.shape[0]
    assert seq % SCORE_CHUNK == 0 and dec_seq == SUBLANES and page_table.shape[1] >= 2

    xp = x_prompt.reshape(batch * seq, d)
    xs = x_sample.reshape(n_dec * dec_seq, d)

    wgu = ffn_w_gu.astype(BF16)
    wdn = ffn_w_down.astype(BF16)
    qi_end = 3 * a + IDX_HEADS * IDX_DIM
    wqkv = attn_w_in[:, :3 * a].astype(BF16)
    wqi = attn_w_in[:, 3 * a:qi_end].astype(BF16)
    wkw = jnp.pad(attn_w_in[:, qi_end:], ((0, 0), (0, LANES - (IDX_DIM + IDX_HEADS)))).astype(BF16)
    wo = attn_w_out.astype(BF16)
    lwin = lru_w_in.astype(BF16)
    lwout = lru_w_out.astype(BF16)
    wax = jnp.concatenate([lru_w_a, lru_w_x], axis=-1).astype(BF16)
    bax = jnp.concatenate([lru_b_a, lru_b_x], axis=-1)[:, None, :]

    xp = _ffn(xp, norm_g[0, 0], wgu[0, 0], wdn[0, 0])
    xs = _ffn(xs, norm_g[0, 0], wgu[0, 0], wdn[0, 0])

    q_p, k_p, v_p, kb_p, vb_p, qi_p, kw_p = _attn_proj(xp, norm_g[0, 1], wqkv, wqi, wkw)
    q_s, k_s, v_s, _, _, qi_s, kw_s = _attn_proj(xs, norm_g[0, 1], wqkv, wqi, wkw)

    nqb = seq // Q_BLOCK
    qi4 = qi_p.reshape(batch, nqb, Q_BLOCK, IDX_HEADS, IDX_DIM).transpose(0, 1, 3, 2, 4)
    qi4 = qi4.reshape(batch, nqb, IDX_HEADS * Q_BLOCK, IDX_DIM)
    kit = kw_p[:, :IDX_DIM].astype(BF16).reshape(batch, seq, IDX_DIM).transpose(0, 2, 1)
    key_pad = ((0, 0), (0, Q_BLOCK), (0, 0))
    kt = jnp.pad(kb_p.reshape(batch, seq, a), key_pad).transpose(0, 2, 1)
    vb3 = jnp.pad(vb_p.reshape(batch, seq, a), key_pad)
    i_q = np.arange(Q_BLOCK)[:, None]
    c_k = np.arange(3 * Q_BLOCK)[None, :]
    toe = _bias_rows(rel_bias, Q_BLOCK + i_q - c_k)
    toe = toe.reshape(N_PAIRS, 2 * Q_BLOCK, 3 * Q_BLOCK)
    xp = _attn_prompt(xp, q_p, kt, vb3, qi4, kw_p, kit, toe, wo, batch, seq)

    qi_s3 = qi_s.reshape(n_dec, dec_seq, IDX_HEADS, IDX_DIM).transpose(0, 2, 1, 3)
    qi_s3 = qi_s3.reshape(n_dec, IDX_HEADS * dec_seq, IDX_DIM)
    wi_s = kw_s[:, IDX_DIM:IDX_DIM + IDX_HEADS].reshape(n_dec, dec_seq, IDX_HEADS) * (IDX_HEADS ** -0.5)
    wb = jnp.broadcast_to(wi_s.transpose(0, 2, 1).reshape(n_dec, IDX_HEADS * dec_seq, 1),
                          (n_dec, IDX_HEADS * dec_seq, LANES))
    pad_rows = ((0, 0), (0, PAGE_SIZE - dec_seq), (0, 0))
    ki_new = jnp.pad(kw_s[:, :IDX_DIM].reshape(n_dec, dec_seq, IDX_DIM), pad_rows)
    scores_past, scores_new = _sample_scores(page_table, qi_s3, wb, cache_kidx, ki_new, dec_seq)
    new_pad = ((0, 0), (0, PAGE_SIZE - dec_seq), (0, 0), (0, 0))
    k_new = jnp.pad(k_s.reshape(n_dec, dec_seq, N_HEADS, HEAD_DIM), new_pad)
    v_new = jnp.pad(v_s.reshape(n_dec, dec_seq, N_HEADS, HEAD_DIM), new_pad)
    tok = np.arange(dec_seq)[:, None]
    jj = np.arange(PAGE_SIZE)[None, :]
    b_last = _bias_rows(rel_bias, PAGE_SIZE + tok - jj)
    b_new = _bias_rows(rel_bias, tok - jj)
    bias3 = jnp.stack([jnp.zeros_like(b_last), b_last, b_new]).reshape(3, N_HEADS * dec_seq, PAGE_SIZE)
    xs3 = _attn_sample(page_table, scores_past, scores_new, q_s.astype(F32).reshape(n_dec, dec_seq, a),
                       cache_k, cache_v, k_new, v_new, bias3, xs.reshape(n_dec, dec_seq, d), wo, dec_seq)
    xs = xs3.reshape(n_dec * dec_seq, d)

    xp = _ffn(xp, norm_g[0, 2], wgu[0, 1], wdn[0, 1])
    xs = _ffn(xs, norm_g[0, 2], wgu[0, 1], wdn[0, 1])

    xp = _ffn(xp, norm_g[1, 0], wgu[1, 0], wdn[1, 0])
    xs = _ffn(xs, norm_g[1, 0], wgu[1, 0], wdn[1, 0])

    lru_args = (lwin, lru_conv_w, lru_conv_b, wax, bax, lru_lambda, lwout)
    xp, conv_p8, h_p8 = _lru(xp, norm_g[1, 1], *lru_args, batch=batch, seq=seq)
    firsts = []
    for k in range(1, CONV_WIDTH):
        r = jnp.pad(state_conv[:, CONV_WIDTH - 1 - k:, :], ((0, 0), (0, dec_seq - k), (0, 0)))
        firsts.append(r.reshape(n_dec * dec_seq, w))
    h0 = jnp.broadcast_to(state_h[:, None, :], (n_dec, dec_seq, w)).reshape(n_dec * dec_seq, w)
    xs, xr_s, hs_s = _lru(xs, norm_g[1, 1], *lru_args, sample_state=firsts + [h0])

    xp = _ffn(xp, norm_g[1, 2], wgu[1, 1], wdn[1, 1], gf=final_norm_g)
    xs = _ffn(xs, norm_g[1, 2], wgu[1, 1], wdn[1, 1], gf=final_norm_g)

    nh, hd = N_HEADS, HEAD_DIM
    keep = CONV_WIDTH - 1
    return (
        xp.reshape(batch, seq, d),
        xs.reshape(n_dec, dec_seq, d),
        k_p.reshape(batch, seq, nh, hd),
        v_p.reshape(batch, seq, nh, hd),
        kw_p[:, :IDX_DIM].reshape(batch, seq, IDX_DIM),
        k_s.reshape(n_dec, dec_seq, nh, hd),
        v_s.reshape(n_dec, dec_seq, nh, hd),
        kw_s[:, :IDX_DIM].reshape(n_dec, dec_seq, IDX_DIM),
        conv_p8[:, SUBLANES - keep:, :],
        h_p8[:, SUBLANES - 1, :],
        xr_s.reshape(n_dec, dec_seq, w)[:, dec_seq - keep:, :],
        hs_s.reshape(n_dec, dec_seq, w)[:, dec_seq - 1, :],
    )
```

```python
import functools
import math

import numpy as np
import jax
import jax.numpy as jnp
from jax import lax
from jax.experimental import pallas as pl
from jax.experimental.pallas import tpu as pltpu

F32 = jnp.float32
BF16 = jnp.bfloat16
NEG_INF = float("-inf")

N_HEADS = 16
HEAD_DIM = 64
ATTN_WIDTH = N_HEADS * HEAD_DIM
IDX_HEADS = 8
IDX_DIM = 64
TOPK_MAX = 256
Q_BLOCK = 128
PAGE_SIZE = 128
N_BUCKETS = 32
MAX_DISTANCE = 128
LRU_BLOCKS = 8
CONV_WIDTH = 4
LRU_C = 8.0
RMS_EPS = 1e-6

V7X_VMEM_BYTES = 64 * 1024 * 1024
LANES = 128
SUBLANES = 8

N_PAIRS = N_HEADS // 2
SCORE_CHUNK = 512
WIDE_CHUNK = 512
SCORE_PAGES = 8
ATTN_PAGES = 4
TAIL_BLOCKS = 5
MAX_BISECT_ITERS = 320


def _vmem_limit(nbytes):
    return int(min(nbytes, V7X_VMEM_BYTES - 4 * 1024 * 1024))


def _rms(x, g):
    ms = jnp.mean(x * x, axis=-1, keepdims=True)
    return x * lax.rsqrt(ms + RMS_EPS) * g


def _sigmoid(x):
    return 1.0 / (1.0 + jnp.exp(-x))


def _dot(a, b):
    return jnp.dot(a, b, preferred_element_type=F32)


def _dot_nt(a, b):
    return lax.dot_general(a, b, (((1,), (1,)), ((), ())), preferred_element_type=F32)


def _fold_lanes(x, op):
    out = x[:, :LANES]
    for j in range(1, x.shape[1] // LANES):
        out = op(out, x[:, j * LANES:(j + 1) * LANES])
    return out


def _ffn_body(*refs, d_ff, fc, final):
    if final:
        x_ref, g_ref, wgu_ref, wd_ref, gf_ref, o_ref = refs
    else:
        x_ref, g_ref, wgu_ref, wd_ref, o_ref = refs
    x = x_ref[...]
    h = _rms(x, g_ref[...]).astype(BF16)
    acc = jnp.zeros(x.shape, F32)
    for c in range(d_ff // fc):
        gg = _dot(h, wgu_ref[:, c * fc:(c + 1) * fc])
        uu = _dot(h, wgu_ref[:, d_ff + c * fc:d_ff + (c + 1) * fc])
        a = (gg * _sigmoid(gg) * uu).astype(BF16)
        acc = acc + _dot(a, wd_ref[c * fc:(c + 1) * fc, :])
    y = x + 0.5 * acc
    if final:
        y = _rms(y, gf_ref[...])
    o_ref[...] = y


def _ffn(x, g, wgu, wd, gf=None):
    t, d = x.shape
    d_ff = wd.shape[0]
    tm = 512 if t % 512 == 0 else 256
    fc = 256
    final = gf is not None
    const = lambda i: (0, 0)
    in_specs = [
        pl.BlockSpec((tm, d), lambda i: (i, 0)),
        pl.BlockSpec((1, d), const),
        pl.BlockSpec((d, 2 * d_ff), const, pipeline_mode=pl.Buffered(1)),
        pl.BlockSpec((d_ff, d), const, pipeline_mode=pl.Buffered(1)),
    ]
    args = [x, g.reshape(1, d), wgu, wd]
    if final:
        in_specs.append(pl.BlockSpec((1, d), const))
        args.append(gf.reshape(1, d))
    return pl.pallas_call(
        functools.partial(_ffn_body, d_ff=d_ff, fc=fc, final=final),
        grid=(t // tm,),
        in_specs=in_specs,
        out_specs=pl.BlockSpec((tm, d), lambda i: (i, 0)),
        out_shape=jax.ShapeDtypeStruct((t, d), F32),
        compiler_params=pltpu.CompilerParams(
            dimension_semantics=("arbitrary",),
            vmem_limit_bytes=_vmem_limit(48 * 1024 * 1024)),
        name="ffn_final" if final else "ffn",
    )(*args)


def _proj_body(x_ref, g_ref, wqkv_ref, wqi_ref, wkw_ref,
               q_ref, k_ref, v_ref, kb_ref, vb_ref, qi_ref, kw_ref):
    h = _rms(x_ref[...], g_ref[...]).astype(BF16)
    a = ATTN_WIDTH
    q_ref[...] = (_dot(h, wqkv_ref[:, :a]) * (HEAD_DIM ** -0.5)).astype(BF16)
    k = _dot(h, wqkv_ref[:, a:2 * a])
    k_ref[...] = k
    kb_ref[...] = k.astype(BF16)
    v = _dot(h, wqkv_ref[:, 2 * a:])
    v_ref[...] = v
    vb_ref[...] = v.astype(BF16)
    qi_ref[...] = (_dot(h, wqi_ref[...]) * (IDX_DIM ** -0.5)).astype(BF16)
    kw_ref[...] = _dot(h, wkw_ref[...])


def _attn_proj(x, g, wqkv, wqi, wkw):
    t, d = x.shape
    tm = 256
    a = ATTN_WIDTH
    nqi = IDX_HEADS * IDX_DIM
    const = lambda i: (0, 0)
    row = lambda i: (i, 0)
    return pl.pallas_call(
        _proj_body,
        grid=(t // tm,),
        in_specs=[
            pl.BlockSpec((tm, d), row),
            pl.BlockSpec((1, d), const),
            pl.BlockSpec((d, 3 * a), const, pipeline_mode=pl.Buffered(1)),
            pl.BlockSpec((d, nqi), const, pipeline_mode=pl.Buffered(1)),
            pl.BlockSpec((d, LANES), const, pipeline_mode=pl.Buffered(1)),
        ],
        out_specs=[
            pl.BlockSpec((tm, a), row), pl.BlockSpec((tm, a), row), pl.BlockSpec((tm, a), row),
            pl.BlockSpec((tm, a), row), pl.BlockSpec((tm, a), row),
            pl.BlockSpec((tm, nqi), row), pl.BlockSpec((tm, LANES), row),
        ],
        out_shape=[
            jax.ShapeDtypeStruct((t, a), BF16), jax.ShapeDtypeStruct((t, a), F32),
            jax.ShapeDtypeStruct((t, a), F32), jax.ShapeDtypeStruct((t, a), BF16),
            jax.ShapeDtypeStruct((t, a), BF16), jax.ShapeDtypeStruct((t, nqi), BF16),
            jax.ShapeDtypeStruct((t, LANES), F32),
        ],
        compiler_params=pltpu.CompilerParams(
            dimension_semantics=("arbitrary",),
            vmem_limit_bytes=_vmem_limit(40 * 1024 * 1024)),
        name="attn_proj",
    )(x, g.reshape(1, d), wqkv, wqi, wkw)


def _select_topk(count, smin, smax, n_adm, n_sel, n_cols):
    ksel = float(n_sel)
    c_hi = count(lambda s, col: s >= smax)
    few = n_adm <= ksel
    top_tied = jnp.logical_and(jnp.logical_not(few), c_hi >= ksel)
    lo0 = jnp.where(top_tied, smax, smin)
    done0 = jnp.where(jnp.logical_or(few, top_tied), 1.0, 0.0)

    def not_all_done(done):
        return (jnp.min(done) < 0.5).astype(jnp.int32)

    def cond(carry):
        return jnp.logical_and(carry[3] > 0, carry[4] < MAX_BISECT_ITERS)

    def body(carry):
        lo, hi, done, _, it = carry
        mid = lo * 0.5 + hi * 0.5
        conv = jnp.logical_or(mid <= lo, mid >= hi)
        c = count(lambda s, col: s >= mid)
        ge = c >= ksel
        upd = jnp.logical_and(done < 0.5, jnp.logical_not(conv))
        lo = jnp.where(jnp.logical_and(upd, ge), mid, lo)
        hi = jnp.where(jnp.logical_and(upd, jnp.logical_not(ge)), mid, hi)
        done = jnp.where(jnp.logical_or(conv, c == ksel), 1.0, done)
        return lo, hi, done, not_all_done(done), it + 1

    thr = lax.while_loop(cond, body, (lo0, smax, done0, not_all_done(done0), jnp.int32(0)))[0]

    c_ge = count(lambda s, col: s >= thr)
    has_tie = c_ge > ksel
    big = float(n_cols)

    def tie_break(_):
        c_gt = count(lambda s, col: s > thr)
        need = ksel - c_gt
        jlo = jnp.full_like(thr, -1.0)
        jhi = jnp.full_like(thr, big)

        def jbody(_, jc):
            jlo, jhi = jc
            jm = jnp.floor((jlo + jhi) * 0.5)
            c = count(lambda s, col: jnp.logical_and(s == thr, col <= jm))
            ok = c >= need
            return jnp.where(ok, jlo, jm), jnp.where(ok, jm, jhi)

        n_it = int(math.ceil(math.log2(n_cols + 2))) + 1
        _, jhi = lax.fori_loop(0, n_it, jbody, (jlo, jhi))
        return jnp.where(has_tie, jhi, big)

    any_tie = jnp.max(jnp.where(has_tie, 1.0, 0.0)) > 0.5
    jmax = lax.cond(any_tie, tie_break, lambda _: jnp.full_like(thr, big), 0)
    return thr, jmax


def _t5_bucket_np(rel):
    n = np.maximum(rel, 0)
    max_exact = N_BUCKETS // 2
    nf = np.maximum(n, max_exact).astype(np.float32)
    large = max_exact + (np.log(nf / np.float32(max_exact)) / np.float32(math.log(MAX_DISTANCE / max_exact))
                         * (N_BUCKETS - max_exact)).astype(np.int32)
    large = np.minimum(large, N_BUCKETS - 1)
    return np.where(n < max_exact, n, large)


_FAR_BUCKET = int(_t5_bucket_np(np.array([1 << 20]))[0])
assert int(_t5_bucket_np(np.array([Q_BLOCK + 1]))[0]) == _FAR_BUCKET


def _bias_rows(rel_bias, rel):
    bucket = _t5_bucket_np(rel)
    b = jnp.take(rel_bias, jnp.asarray(bucket.reshape(-1)), axis=0).reshape(rel.shape + (N_HEADS,))
    b = b - rel_bias[_FAR_BUCKET][None, None, :]
    return jnp.transpose(b, (2, 0, 1))


def _attn_prompt_body(qi_ref, kw_ref, kit_ref, q_ref, kt_ref, v_ref, toe_ref, x_ref, wo_ref,
                      o_ref, s_ref, lg_ref, oacc_ref, *, seq, n_sel):
    qb = pl.program_id(1)
    qn = Q_BLOCK
    sc = SCORE_CHUNK
    n_sc = (qb + sc // qn) // (sc // qn)

    qi = qi_ref[0, 0]
    wi = kw_ref[:, IDX_DIM:IDX_DIM + IDX_HEADS] * (IDX_HEADS ** -0.5)
    wcols = [jnp.broadcast_to(wi[:, h:h + 1], (qn, sc)) for h in range(IDX_HEADS)]
    qpos = lax.broadcasted_iota(jnp.int32, (qn, sc), 0) + qb * qn
    lane = lax.broadcasted_iota(jnp.int32, (qn, sc), 1)

    def score_chunk(c, carry):
        smin, smax = carry
        c0 = pl.multiple_of(c * sc, sc)
        s = _dot(qi, kit_ref[0, :, pl.ds(c0, sc)])
        tot = jnp.maximum(s[:qn], 0.0) * wcols[0]
        for h in range(1, IDX_HEADS):
            tot = tot + jnp.maximum(s[h * qn:(h + 1) * qn], 0.0) * wcols[h]
        adm = (lane + c0) <= qpos
        s_ref[:, pl.ds(c0, sc)] = jnp.where(adm, tot, NEG_INF)
        smin = jnp.minimum(smin, _fold_lanes(jnp.where(adm, tot, jnp.inf), jnp.minimum))
        smax = jnp.maximum(smax, _fold_lanes(jnp.where(adm, tot, NEG_INF), jnp.maximum))
        return smin, smax

    smin, smax = lax.fori_loop(
        0, n_sc, score_chunk,
        (jnp.full((qn, LANES), jnp.inf, F32), jnp.full((qn, LANES), NEG_INF, F32)))
    smin = jnp.min(smin, axis=1, keepdims=True)
    smax = jnp.max(smax, axis=1, keepdims=True)

    def count(pred):
        def body(c, acc):
            c0 = pl.multiple_of(c * sc, sc)
            s = s_ref[:, pl.ds(c0, sc)]
            col = (lane + c0).astype(F32)
            return acc + _fold_lanes(jnp.where(pred(s, col), 1.0, 0.0), jnp.add)
        acc = lax.fori_loop(0, n_sc, body, jnp.zeros((qn, LANES), F32))
        return jnp.sum(acc, axis=1, keepdims=True)

    n_adm = (lax.broadcasted_iota(jnp.int32, (qn, 1), 0) + qb * qn + 1).astype(F32)
    thr, jmax = _select_topk(count, smin, smax, n_adm, n_sel, seq)

    def mask_chunk(c, _):
        c0 = pl.multiple_of(c * sc, sc)
        s = s_ref[:, pl.ds(c0, sc)]
        col = (lane + c0).astype(F32)
        sel = jnp.logical_or(s > thr, jnp.logical_and(s == thr, col <= jmax))
        sel = jnp.logical_and(sel, s > NEG_INF)
        s_ref[:, pl.ds(c0, sc)] = jnp.where(sel, 0.0, NEG_INF)
        return 0

    lax.fori_loop(0, n_sc, mask_chunk, 0)

    n_far = jnp.maximum(qb - 1, 0)
    wc = WIDE_CHUNK
    n_wide = n_far // (wc // qn)
    tail0 = pl.multiple_of(n_wide * wc, wc)
    tw = TAIL_BLOCKS * qn
    near0 = pl.multiple_of(tail0 + jnp.where(qb > 0, n_far % (wc // qn), 0) * qn, qn)
    toe0 = pl.multiple_of(jnp.where(qb > 0, 0, qn), qn)
    s_ref[:, pl.ds(pl.multiple_of(n_sc * sc, sc), qn)] = jnp.full((qn, qn), NEG_INF, F32)
    lane_p = lax.broadcasted_iota(jnp.int32, (qn, LANES), 1)

    def pair_body(pr, _):
        p0 = pl.multiple_of(pr * LANES, LANES)
        qp = q_ref[:, pl.ds(p0, LANES)]
        zero = jnp.zeros_like(qp)
        qs = jnp.concatenate([jnp.where(lane_p < HEAD_DIM, qp, zero),
                              jnp.where(lane_p >= HEAD_DIM, qp, zero)], axis=0)

        def logits(c0, width):
            lg = _dot(qs, kt_ref[0, pl.ds(p0, LANES), pl.ds(c0, width)])
            m = s_ref[:, pl.ds(c0, width)]
            return lg + jnp.concatenate([m, m], axis=0)

        def p1(c0, width, mrun):
            lg = logits(c0, width)
            lg_ref[:, pl.ds(c0, width)] = lg
            return jnp.maximum(mrun, _fold_lanes(lg, jnp.maximum))

        def p2(c0, width, carry):
            lrun, acc = carry
            p = jnp.exp(lg_ref[:, pl.ds(c0, width)] - m)
            lrun = lrun + _fold_lanes(p, jnp.add)
            acc = acc + _dot(p.astype(BF16), v_ref[0, pl.ds(c0, width), pl.ds(p0, LANES)])
            return lrun, acc

        def over_wide(step, carry):
            done = 0
            for mult in (4, 2, 1):
                width = mult * wc
                trips = n_wide // 4 if mult == 4 else (n_wide // mult) % 2

                def body(i, c, width=width, base=done):
                    return step(pl.multiple_of(base + i * width, wc), width, c)

                carry = lax.fori_loop(0, trips, body, carry)
                done = done + trips * width
            return carry

        mrun = over_wide(p1, jnp.full((2 * qn, LANES), NEG_INF, F32))
        lg_ref[:, pl.ds(tail0, tw)] = logits(tail0, tw)
        lg_ref[:, pl.ds(near0, 2 * qn)] = lg_ref[:, pl.ds(near0, 2 * qn)] + toe_ref[pr, :, pl.ds(toe0, 2 * qn)]
        mrun = jnp.maximum(mrun, _fold_lanes(lg_ref[:, pl.ds(tail0, tw)], jnp.maximum))
        m = jnp.max(mrun, axis=1, keepdims=True)

        carry = over_wide(p2, (jnp.zeros((2 * qn, LANES), F32), jnp.zeros((2 * qn, LANES), F32)))
        lrun, acc = p2(tail0, tw, carry)
        o = acc / jnp.sum(lrun, axis=1, keepdims=True)
        oacc_ref[:, pl.ds(p0, LANES)] = jnp.where(lane_p < HEAD_DIM, o[:qn], o[qn:])
        return 0

    lax.fori_loop(0, N_PAIRS, pair_body, 0)

    o_ref[...] = x_ref[...] + _dot(oacc_ref[...].astype(BF16), wo_ref[...])


def _attn_prompt(x, q, kt, vb, qi4, kw, kit, toe, wo, batch, seq):
    d = x.shape[1]
    a = ATTN_WIDTH
    nqb = seq // Q_BLOCK
    n_sel = min(TOPK_MAX, seq // 4)
    blk = lambda b, i: (b * nqb + i, 0)
    per_b3 = lambda b, i: (b, 0, 0)
    seqp = kt.shape[2]
    assert seqp >= seq + (TAIL_BLOCKS - WIDE_CHUNK // Q_BLOCK) * Q_BLOCK
    resident = 2 * seqp * a * 2 + Q_BLOCK * seqp * 4 + 2 * Q_BLOCK * seqp * 4
    return pl.pallas_call(
        functools.partial(_attn_prompt_body, seq=seq, n_sel=n_sel),
        grid=(batch, nqb),
        in_specs=[
            pl.BlockSpec((1, 1, IDX_HEADS * Q_BLOCK, IDX_DIM), lambda b, i: (b, i, 0, 0)),
            pl.BlockSpec((Q_BLOCK, LANES), blk),
            pl.BlockSpec((1, IDX_DIM, seq), per_b3),
            pl.BlockSpec((Q_BLOCK, a), blk),
            pl.BlockSpec((1, a, seqp), per_b3, pipeline_mode=pl.Buffered(1)),
            pl.BlockSpec((1, seqp, a), per_b3, pipeline_mode=pl.Buffered(1)),
            pl.BlockSpec((N_PAIRS, 2 * Q_BLOCK, 3 * Q_BLOCK), lambda b, i: (0, 0, 0), pipeline_mode=pl.Buffered(1)),
            pl.BlockSpec((Q_BLOCK, d), blk),
            pl.BlockSpec((a, d), lambda b, i: (0, 0), pipeline_mode=pl.Buffered(1)),
        ],
        out_specs=pl.BlockSpec((Q_BLOCK, d), blk),
        out_shape=jax.ShapeDtypeStruct(x.shape, F32),
        scratch_shapes=[
            pltpu.VMEM((Q_BLOCK, seqp), F32),
            pltpu.VMEM((2 * Q_BLOCK, seqp), F32),
            pltpu.VMEM((Q_BLOCK, a), F32),
        ],
        compiler_params=pltpu.CompilerParams(
            dimension_semantics=("arbitrary", "arbitrary"),
            vmem_limit_bytes=_vmem_limit(resident + 14 * 1024 * 1024)),
        name="attn_prompt",
    )(qi4, kw, kit, q, kt, vb, toe, x, wo)


def _sample_scores_body(pt_ref, qi_ref, wb_ref, *refs, n_grp, dec_seq):
    page_refs = refs[:n_grp]
    kinew_ref, o_ref, onew_ref = refs[n_grp:]
    qi = qi_ref[0]
    wb = wb_ref[0]

    def score(keys_t, n_tiles):
        s = _dot(qi, keys_t.astype(BF16))
        t = jnp.maximum(s, 0.0) * jnp.concatenate([wb] * n_tiles, axis=1)
        tot = t[:dec_seq]
        for h in range(1, IDX_HEADS):
            tot = tot + t[h * dec_seq:(h + 1) * dec_seq]
        return tot

    o_ref[0] = score(jnp.concatenate([r[0] for r in page_refs], axis=1), n_grp)

    @pl.when(pl.program_id(1) == 0)
    def _():
        tot = score(kinew_ref[0], 1)
        j = lax.broadcasted_iota(jnp.int32, tot.shape, 1)
        tok = lax.broadcasted_iota(jnp.int32, tot.shape, 0)
        onew_ref[0] = jnp.where(j <= tok, tot, NEG_INF)


def _sample_scores(page_table, qi_s, wb, cache_kidx, ki_new_pad, dec_seq):
    n, n_pages = page_table.shape
    n_grp = SCORE_PAGES if n_pages % SCORE_PAGES == 0 else 1
    rows = IDX_HEADS * dec_seq
    per_seq = lambda i, g, pt: (i, 0, 0)

    def page(j):
        return lambda i, g, pt: (pt[i, g * n_grp + j], 0, 0)

    grid_spec = pltpu.PrefetchScalarGridSpec(
        num_scalar_prefetch=1,
        grid=(n, n_pages // n_grp),
        in_specs=[pl.BlockSpec((1, rows, IDX_DIM), per_seq), pl.BlockSpec((1, rows, LANES), per_seq)]
        + [pl.BlockSpec((1, IDX_DIM, PAGE_SIZE), page(j)) for j in range(n_grp)]
        + [pl.BlockSpec((1, IDX_DIM, PAGE_SIZE), per_seq)],
        out_specs=[pl.BlockSpec((1, dec_seq, n_grp * PAGE_SIZE), lambda i, g, pt: (i, 0, g)),
                   pl.BlockSpec((1, dec_seq, PAGE_SIZE), per_seq)],
    )
    return pl.pallas_call(
        functools.partial(_sample_scores_body, n_grp=n_grp, dec_seq=dec_seq),
        grid_spec=grid_spec,
        out_shape=[jax.ShapeDtypeStruct((n, dec_seq, n_pages * PAGE_SIZE), F32),
                   jax.ShapeDtypeStruct((n, dec_seq, PAGE_SIZE), F32)],
        compiler_params=pltpu.CompilerParams(dimension_semantics=("arbitrary", "arbitrary")),
        name="sample_scores",
    )(page_table, qi_s, wb, *([cache_kidx] * n_grp), ki_new_pad)


def _attn_sample_body(pt_ref, scp_ref, scn_ref, q_ref, *refs, n_pages, n_grp, dec_seq, n_sel):
    k_refs = refs[:n_grp]
    v_refs = refs[n_grp:2 * n_grp]
    (knew_ref, vnew_ref, bias_ref, x_ref, wo_ref,
     o_ref, mask_ref, q16_ref, m_ref, l_ref, acc_ref) = refs[2 * n_grp:]
    g = pl.program_id(1)
    n_groups = n_pages // n_grp
    n_cols = (n_pages + 1) * PAGE_SIZE
    ds = dec_seq

    @pl.when(g == 0)
    def _():
        s = jnp.concatenate([scp_ref[0], scn_ref[0]], axis=1)
        col = lax.broadcasted_iota(jnp.int32, s.shape, 1).astype(F32)
        fin = s > NEG_INF
        smin = jnp.min(jnp.where(fin, s, jnp.inf), axis=1, keepdims=True)
        smax = jnp.max(s, axis=1, keepdims=True)

        def count(pred):
            return jnp.sum(jnp.where(pred(s, col), 1.0, 0.0), axis=1, keepdims=True)

        tok = lax.broadcasted_iota(jnp.int32, (ds, 1), 0)
        n_adm = (tok + n_pages * PAGE_SIZE + 1).astype(F32)
        thr, jmax = _select_topk(count, smin, smax, n_adm, n_sel, n_cols)
        sel = jnp.logical_or(s > thr, jnp.logical_and(s == thr, col <= jmax))
        sel = jnp.logical_and(sel, fin)
        mask_ref[...] = jnp.where(sel, 0.0, NEG_INF)
        q = q_ref[0]
        for h in range(N_HEADS):
            qh = q[:, h * HEAD_DIM:(h + 1) * HEAD_DIM]
            q16_ref[h] = jnp.concatenate([qh, jnp.zeros_like(qh)], axis=0).astype(BF16)
        m_ref[...] = jnp.full(m_ref.shape, NEG_INF, F32)
        l_ref[...] = jnp.zeros(l_ref.shape, F32)
        acc_ref[...] = jnp.zeros(acc_ref.shape, F32)

    def head_t(page_refs, h):
        return jnp.concatenate([r[0, h] for r in page_refs], axis=1).astype(BF16)

    def pad_rows16(x):
        return jnp.concatenate([x, jnp.zeros_like(x)], axis=0).astype(BF16)

    def attend(k_refs, v_refs, mk, bias):
        lg = jnp.concatenate([_dot(q16_ref[h], head_t(k_refs, h))[:ds] for h in range(N_HEADS)], axis=0)
        lg = lg + jnp.concatenate([mk] * N_HEADS, axis=0) + bias
        m_old = m_ref[...]
        m_new = jnp.maximum(m_old, jnp.max(lg, axis=1, keepdims=True))
        m_safe = jnp.where(m_new == NEG_INF, 0.0, m_new)
        alpha = jnp.exp(m_old - m_safe)
        pr = jnp.exp(lg - m_safe)
        l_ref[...] = alpha * l_ref[...] + jnp.sum(pr, axis=1, keepdims=True)
        pv = jnp.concatenate([_dot_nt(pad_rows16(pr[h * ds:(h + 1) * ds]), head_t(v_refs, h))[:ds]
                              for h in range(N_HEADS)], axis=0)
        acc_ref[...] = alpha * acc_ref[...] + pv
        m_ref[...] = m_new

    @pl.when(g < n_groups)
    def _():
        c0 = pl.multiple_of(g * (n_grp * PAGE_SIZE), n_grp * PAGE_SIZE)
        mk = mask_ref[:, pl.ds(c0, n_grp * PAGE_SIZE)]
        near = jnp.where(g == n_groups - 1, bias_ref[1], 0.0)
        if n_grp > 1:
            near = jnp.concatenate([jnp.zeros((near.shape[0], (n_grp - 1) * PAGE_SIZE), F32), near], axis=1)
        attend(k_refs, v_refs, mk, near)

    @pl.when(g == n_groups)
    def _():
        attend([knew_ref], [vnew_ref], mask_ref[:, pl.ds(n_pages * PAGE_SIZE, PAGE_SIZE)], bias_ref[2])
        o = acc_ref[...] / l_ref[...]
        out = jnp.zeros((2 * ds, wo_ref.shape[1]), F32)
        for h in range(N_HEADS):
            out = out + _dot(pad_rows16(o[h * ds:(h + 1) * ds]), wo_ref[h * HEAD_DIM:(h + 1) * HEAD_DIM, :])
        o_ref[0] = x_ref[0] + out[:ds]


def _attn_sample(page_table, scores_past, scores_new, q_s, cache_k, cache_v, k_new, v_new, bias3, x_s, wo, dec_seq):
    n, n_pages = page_table.shape
    n_grp = ATTN_PAGES if n_pages % ATTN_PAGES == 0 else 1
    n_groups = n_pages // n_grp
    a = ATTN_WIDTH
    d = x_s.shape[-1]
    rows = N_HEADS * dec_seq
    page_shape = (1, N_HEADS, HEAD_DIM, PAGE_SIZE)
    n_cols = (n_pages + 1) * PAGE_SIZE
    n_sel = min(TOPK_MAX, (n_pages * PAGE_SIZE + dec_seq) // 4)
    per_seq = lambda i, g, pt: (i, 0, 0)
    per_seq4 = lambda i, g, pt: (i, 0, 0, 0)

    def page(j):
        return lambda i, g, pt: (pt[i, jnp.minimum(g, n_groups - 1) * n_grp + j], 0, 0, 0)

    page_specs = [pl.BlockSpec(page_shape, page(j)) for j in range(n_grp)]
    grid_spec = pltpu.PrefetchScalarGridSpec(
        num_scalar_prefetch=1,
        grid=(n, n_groups + 1),
        in_specs=[
            pl.BlockSpec((1, dec_seq, n_pages * PAGE_SIZE), per_seq),
            pl.BlockSpec((1, dec_seq, PAGE_SIZE), per_seq),
            pl.BlockSpec((1, dec_seq, a), per_seq),
        ] + page_specs + page_specs + [
            pl.BlockSpec(page_shape, per_seq4),
            pl.BlockSpec(page_shape, per_seq4),
            pl.BlockSpec((3, rows, PAGE_SIZE), lambda i, g, pt: (0, 0, 0)),
            pl.BlockSpec((1, dec_seq, d), per_seq),
            pl.BlockSpec((a, d), lambda i, g, pt: (0, 0)),
        ],
        out_specs=pl.BlockSpec((1, dec_seq, d), per_seq),
        scratch_shapes=[
            pltpu.VMEM((dec_seq, n_cols), F32),
            pltpu.VMEM((N_HEADS, 2 * dec_seq, HEAD_DIM), BF16),
            pltpu.VMEM((rows, 1), F32),
            pltpu.VMEM((rows, 1), F32),
            pltpu.VMEM((rows, HEAD_DIM), F32),
        ],
    )
    return pl.pallas_call(
        functools.partial(_attn_sample_body, n_pages=n_pages, n_grp=n_grp, dec_seq=dec_seq, n_sel=n_sel),
        grid_spec=grid_spec,
        out_shape=jax.ShapeDtypeStruct(x_s.shape, F32),
        compiler_params=pltpu.CompilerParams(
            dimension_semantics=("arbitrary", "arbitrary"),
            vmem_limit_bytes=_vmem_limit(44 * 1024 * 1024)),
        name="attn_sample",
    )(page_table, scores_past, scores_new, q_s, *([cache_k] * n_grp), *([cache_v] * n_grp),
      k_new, v_new, bias3, x_s, wo)


def _gelu_tanh(x):
    return 0.5 * x * (1.0 + jnp.tanh(math.sqrt(2.0 / math.pi) * (x + 0.044715 * (x * x * x))))


def _lru_body(*refs, sample, tm, width):
    if sample:
        (x_ref, g_ref, win_ref, cw_ref, cb_ref, wax_ref, bax_ref, lam_ref, wout_ref,
         r1_ref, r2_ref, r3_ref, h0_ref, o_ref, xr_ref, hs_ref, a_ref, u_ref) = refs
    else:
        (x_ref, g_ref, win_ref, cw_ref, cb_ref, wax_ref, bax_ref, lam_ref, wout_ref,
         o_ref, xr_ref, hs_ref, a_ref, u_ref, prev_ref, hc_ref) = refs

        @pl.when(pl.program_id(1) == 0)
        def _():
            prev_ref[...] = jnp.zeros(prev_ref.shape, F32)
            hc_ref[...] = jnp.zeros(hc_ref.shape, F32)

    w = width
    x = x_ref[...]
    h = _rms(x, g_ref[...]).astype(BF16)
    z = _dot(h, win_ref[...])
    gate = _gelu_tanh(z[:, :w])
    xr = z[:, w:]
    row = lax.broadcasted_iota(jnp.int32, (tm, w), 0)
    pos8 = row & (SUBLANES - 1)

    shifted = []
    for k in range(1, CONV_WIDTH):
        rolled = pltpu.roll(xr, k, 0)
        if sample:
            first = (r1_ref, r2_ref, r3_ref)[k - 1][...]
            shifted.append(jnp.where(pos8 < k, first, rolled))
        else:
            first = jnp.concatenate([pltpu.roll(prev_ref[...], k, 0)] * (tm // SUBLANES), axis=0)
            shifted.append(jnp.where(row < k, first, rolled))
    xc = cb_ref[...]
    for j in range(CONV_WIDTH - 1):
        xc = xc + shifted[CONV_WIDTH - 2 - j] * cw_ref[j:j + 1, :]
    xc = xc + xr * cw_ref[CONV_WIDTH - 1:CONV_WIDTH, :]

    lam = lam_ref[...]
    nl = -lam
    softplus = jnp.maximum(nl, 0.0) + jnp.log1p(jnp.exp(-jnp.abs(nl)))
    xcb = xc.astype(BF16)
    bw = w // LRU_BLOCKS
    for gi in range(LRU_BLOCKS):
        sl = slice(gi * bw, (gi + 1) * bw)
        ga = _dot(xcb[:, sl], wax_ref[gi]) + bax_ref[gi]
        r = _sigmoid(ga[:, :bw])
        ig = _sigmoid(ga[:, bw:])
        log_a = -LRU_C * r * softplus[:, sl]
        a_ref[:, sl] = jnp.exp(log_a)
        th = jnp.tanh(log_a)
        u_ref[:, sl] = jnp.sqrt(-2.0 * th / (1.0 - th)) * (ig * xc[:, sl])

    a_c = a_ref[...]
    u_c = u_ref[...]
    for s in (1, 2, 4):
        ok = pos8 >= s
        a_sh = jnp.where(ok, pltpu.roll(a_c, s, 0), 1.0)
        u_sh = jnp.where(ok, pltpu.roll(u_c, s, 0), 0.0)
        u_c = u_c + a_c * u_sh
        a_c = a_c * a_sh
    if sample:
        hs = u_c + a_c * h0_ref[...]
    else:
        carry = hc_ref[...]
        parts = []
        for gi in range(tm // SUBLANES):
            sl = slice(gi * SUBLANES, (gi + 1) * SUBLANES)
            hg = u_c[sl] + a_c[sl] * carry
            parts.append(hg)
            carry = jnp.broadcast_to(hg[SUBLANES - 1:SUBLANES], (SUBLANES, w))
        hs = jnp.concatenate(parts, axis=0)
        hc_ref[...] = carry
        prev_ref[...] = xr[tm - SUBLANES:]

    y = (hs * gate).astype(BF16)
    o_ref[...] = x + _dot(y, wout_ref[...])
    if sample:
        xr_ref[...] = xr
        hs_ref[...] = hs
    else:
        xr_ref[0] = xr[tm - SUBLANES:]
        hs_ref[0] = hs[tm - SUBLANES:]


def _lru(x, g, win, cw, cb, wax, bax, lam, wout, batch=None, seq=None, sample_state=None):
    t, d = x.shape
    w = wout.shape[0]
    bw = w // LRU_BLOCKS
    sample = sample_state is not None
    const2 = lambda *_: (0, 0)
    const3 = lambda *_: (0, 0, 0)
    single = pl.Buffered(1)
    if sample:
        tm = t
        grid = (1,)
        row = lambda i: (0, 0)
        dims = ("arbitrary",)
    else:
        tm = 256
        nt = seq // tm
        grid = (batch, nt)
        row = lambda b, i: (b * nt + i, 0)
        dims = ("arbitrary", "arbitrary")
    in_specs = [
        pl.BlockSpec((tm, d), row),
        pl.BlockSpec((1, d), const2),
        pl.BlockSpec((d, 2 * w), const2, pipeline_mode=single),
        pl.BlockSpec((CONV_WIDTH, w), const2),
        pl.BlockSpec((1, w), const2),
        pl.BlockSpec((LRU_BLOCKS, bw, 2 * bw), const3),
        pl.BlockSpec((LRU_BLOCKS, 1, 2 * bw), const3),
        pl.BlockSpec((1, w), const2),
        pl.BlockSpec((w, d), const2, pipeline_mode=single),
    ]
    args = [x, g.reshape(1, d), win, cw, cb.reshape(1, w), wax, bax, lam.reshape(1, w), wout]
    scratch = [pltpu.VMEM((tm, w), F32), pltpu.VMEM((tm, w), F32)]
    if sample:
        in_specs += [pl.BlockSpec((tm, w), row)] * 4
        args += list(sample_state)
        out_specs = [pl.BlockSpec((tm, d), row), pl.BlockSpec((tm, w), row), pl.BlockSpec((tm, w), row)]
        out_shape = [jax.ShapeDtypeStruct((t, d), F32), jax.ShapeDtypeStruct((t, w), F32),
                     jax.ShapeDtypeStruct((t, w), F32)]
    else:
        tail = lambda b, i: (b, 0, 0)
        out_specs = [pl.BlockSpec((tm, d), row), pl.BlockSpec((1, SUBLANES, w), tail),
                     pl.BlockSpec((1, SUBLANES, w), tail)]
        out_shape = [jax.ShapeDtypeStruct((t, d), F32), jax.ShapeDtypeStruct((batch, SUBLANES, w), F32),
                     jax.ShapeDtypeStruct((batch, SUBLANES, w), F32)]
        scratch += [pltpu.VMEM((SUBLANES, w), F32), pltpu.VMEM((SUBLANES, w), F32)]
    return pl.pallas_call(
        functools.partial(_lru_body, sample=sample, tm=tm, width=w),
        grid=grid,
        in_specs=in_specs,
        out_specs=out_specs,
        out_shape=out_shape,
        scratch_shapes=scratch,
        compiler_params=pltpu.CompilerParams(
            dimension_semantics=dims,
            vmem_limit_bytes=_vmem_limit(40 * 1024 * 1024)),
        name="lru_sample" if sample else "lru_prompt",
    )(*args)


def kernel(x_prompt, x_sample, cache_k, cache_v, cache_kidx, page_table, state_conv, state_h, rel_bias, attn_w_in, attn_w_out, lru_w_in, lru_conv_w, lru_conv_b, lru_w_a, lru_b_a, lru_w_x, lru_b_x, lru_lambda, lru_w_out, norm_g, ffn_w_gu, ffn_w_down, final_norm_g):
    batch, seq, d = x_prompt.shape
    n_dec, dec_seq, _ = x_sample.shape
    n_pages = page_table.shape[1]
    n_pool = cache_k.shape[0]
    a = ATTN_WIDTH
    w = lru_w_out.shape[0]
    assert seq % SCORE_CHUNK == 0 and dec_seq == SUBLANES and page_table.shape[1] >= 2

    xp = x_prompt.reshape(batch * seq, d)
    xs = x_sample.reshape(n_dec * dec_seq, d)

    wgu = ffn_w_gu.astype(BF16)
    wdn = ffn_w_down.astype(BF16)
    qi_end = 3 * a + IDX_HEADS * IDX_DIM
    wqkv = attn_w_in[:, :3 * a].astype(BF16)
    wqi = attn_w_in[:, 3 * a:qi_end].astype(BF16)
    wkw = jnp.pad(attn_w_in[:, qi_end:], ((0, 0), (0, LANES - (IDX_DIM + IDX_HEADS)))).astype(BF16)
    wo = attn_w_out.astype(BF16)
    lwin = lru_w_in.astype(BF16)
    lwout = lru_w_out.astype(BF16)
    wax = jnp.concatenate([lru_w_a, lru_w_x], axis=-1).astype(BF16)
    bax = jnp.concatenate([lru_b_a, lru_b_x], axis=-1)[:, None, :]

    xp = _ffn(xp, norm_g[0, 0], wgu[0, 0], wdn[0, 0])
    xs = _ffn(xs, norm_g[0, 0], wgu[0, 0], wdn[0, 0])

    q_p, k_p, v_p, kb_p, vb_p, qi_p, kw_p = _attn_proj(xp, norm_g[0, 1], wqkv, wqi, wkw)
    q_s, k_s, v_s, _, _, qi_s, kw_s = _attn_proj(xs, norm_g[0, 1], wqkv, wqi, wkw)

    nqb = seq // Q_BLOCK
    qi4 = qi_p.reshape(batch, nqb, Q_BLOCK, IDX_HEADS, IDX_DIM).transpose(0, 1, 3, 2, 4)
    qi4 = qi4.reshape(batch, nqb, IDX_HEADS * Q_BLOCK, IDX_DIM)
    kit = kw_p[:, :IDX_DIM].astype(BF16).reshape(batch, seq, IDX_DIM).transpose(0, 2, 1)
    key_pad = ((0, 0), (0, Q_BLOCK), (0, 0))
    kt = jnp.pad(kb_p.reshape(batch, seq, a), key_pad).transpose(0, 2, 1)
    vb3 = jnp.pad(vb_p.reshape(batch, seq, a), key_pad)
    i_q = np.arange(Q_BLOCK)[:, None]
    c_k = np.arange(3 * Q_BLOCK)[None, :]
    toe = _bias_rows(rel_bias, Q_BLOCK + i_q - c_k)
    toe = toe.reshape(N_PAIRS, 2 * Q_BLOCK, 3 * Q_BLOCK)
    xp = _attn_prompt(xp, q_p, kt, vb3, qi4, kw_p, kit, toe, wo, batch, seq)

    qi_s3 = qi_s.reshape(n_dec, dec_seq, IDX_HEADS, IDX_DIM).transpose(0, 2, 1, 3)
    qi_s3 = qi_s3.reshape(n_dec, IDX_HEADS * dec_seq, IDX_DIM)
    wi_s = kw_s[:, IDX_DIM:IDX_DIM + IDX_HEADS].reshape(n_dec, dec_seq, IDX_HEADS) * (IDX_HEADS ** -0.5)
    wb = jnp.broadcast_to(wi_s.transpose(0, 2, 1).reshape(n_dec, IDX_HEADS * dec_seq, 1),
                          (n_dec, IDX_HEADS * dec_seq, LANES))
    new_keys = PAGE_SIZE - dec_seq
    ki_new = jnp.pad(kw_s[:, :IDX_DIM].reshape(n_dec, dec_seq, IDX_DIM).transpose(0, 2, 1),
                     ((0, 0), (0, 0), (0, new_keys)))
    scores_past, scores_new = _sample_scores(page_table, qi_s3, wb, cache_kidx.transpose(0, 2, 1), ki_new, dec_seq)
    new_pad = ((0, 0), (0, 0), (0, 0), (0, new_keys))
    k_new = jnp.pad(k_s.reshape(n_dec, dec_seq, N_HEADS, HEAD_DIM).transpose(0, 2, 3, 1), new_pad)
    v_new = jnp.pad(v_s.reshape(n_dec, dec_seq, N_HEADS, HEAD_DIM).transpose(0, 2, 3, 1), new_pad)
    tok = np.arange(dec_seq)[:, None]
    jj = np.arange(PAGE_SIZE)[None, :]
    b_last = _bias_rows(rel_bias, PAGE_SIZE + tok - jj)
    b_new = _bias_rows(rel_bias, tok - jj)
    bias3 = jnp.stack([jnp.zeros_like(b_last), b_last, b_new]).reshape(3, N_HEADS * dec_seq, PAGE_SIZE)
    xs3 = _attn_sample(page_table, scores_past, scores_new, q_s.astype(F32).reshape(n_dec, dec_seq, a),
                       cache_k.transpose(0, 2, 3, 1), cache_v.transpose(0, 2, 3, 1), k_new, v_new, bias3, xs.reshape(n_dec, dec_seq, d), wo, dec_seq)
    xs = xs3.reshape(n_dec * dec_seq, d)

    xp = _ffn(xp, norm_g[0, 2], wgu[0, 1], wdn[0, 1])
    xs = _ffn(xs, norm_g[0, 2], wgu[0, 1], wdn[0, 1])

    xp = _ffn(xp, norm_g[1, 0], wgu[1, 0], wdn[1, 0])
    xs = _ffn(xs, norm_g[1, 0], wgu[1, 0], wdn[1, 0])

    lru_args = (lwin, lru_conv_w, lru_conv_b, wax, bax, lru_lambda, lwout)
    xp, conv_p8, h_p8 = _lru(xp, norm_g[1, 1], *lru_args, batch=batch, seq=seq)
    firsts = []
    for k in range(1, CONV_WIDTH):
        r = jnp.pad(state_conv[:, CONV_WIDTH - 1 - k:, :], ((0, 0), (0, dec_seq - k), (0, 0)))
        firsts.append(r.reshape(n_dec * dec_seq, w))
    h0 = jnp.broadcast_to(state_h[:, None, :], (n_dec, dec_seq, w)).reshape(n_dec * dec_seq, w)
    xs, xr_s, hs_s = _lru(xs, norm_g[1, 1], *lru_args, sample_state=firsts + [h0])

    xp = _ffn(xp, norm_g[1, 2], wgu[1, 1], wdn[1, 1], gf=final_norm_g)
    xs = _ffn(xs, norm_g[1, 2], wgu[1, 1], wdn[1, 1], gf=final_norm_g)

    nh, hd = N_HEADS, HEAD_DIM
    keep = CONV_WIDTH - 1
    return (
        xp.reshape(batch, seq, d),
        xs.reshape(n_dec, dec_seq, d),
        k_p.reshape(batch, seq, nh, hd),
        v_p.reshape(batch, seq, nh, hd),
        kw_p[:, :IDX_DIM].reshape(batch, seq, IDX_DIM),
        k_s.reshape(n_dec, dec_seq, nh, hd),
        v_s.reshape(n_dec, dec_seq, nh, hd),
        kw_s[:, :IDX_DIM].reshape(n_dec, dec_seq, IDX_DIM),
        conv_p8[:, SUBLANES - keep:, :],
        h_p8[:, SUBLANES - 1, :],
        xr_s.reshape(n_dec, dec_seq, w)[:, dec_seq - keep:, :],
        hs_s.reshape(n_dec, dec_seq, w)[:, dec_seq - 1, :],
    )
```

```python
import functools
import math

import numpy as np
import jax
import jax.numpy as jnp
from jax import lax
from jax.experimental import pallas as pl
from jax.experimental.pallas import tpu as pltpu

F32 = jnp.float32
BF16 = jnp.bfloat16
NEG_INF = float("-inf")

N_HEADS = 16
HEAD_DIM = 64
ATTN_WIDTH = N_HEADS * HEAD_DIM
IDX_HEADS = 8
IDX_DIM = 64
TOPK_MAX = 256
Q_BLOCK = 128
PAGE_SIZE = 128
N_BUCKETS = 32
MAX_DISTANCE = 128
LRU_BLOCKS = 8
CONV_WIDTH = 4
LRU_C = 8.0
RMS_EPS = 1e-6

V7X_VMEM_BYTES = 64 * 1024 * 1024
LANES = 128
SUBLANES = 8

N_PAIRS = N_HEADS // 2
SCORE_CHUNK = 512
WIDE_CHUNK = 512
SCORE_PAGES = 8
ATTN_PAGES = 8
TAIL_BLOCKS = 5
MAX_BISECT_ITERS = 320


def _vmem_limit(nbytes):
    return int(min(nbytes, V7X_VMEM_BYTES - 4 * 1024 * 1024))


def _rms(x, g):
    ms = jnp.mean(x * x, axis=-1, keepdims=True)
    return x * lax.rsqrt(ms + RMS_EPS) * g


def _sigmoid(x):
    return 1.0 / (1.0 + jnp.exp(-x))


def _dot(a, b):
    return jnp.dot(a, b, preferred_element_type=F32)


def _dot_nt(a, b):
    return lax.dot_general(a, b, (((1,), (1,)), ((), ())), preferred_element_type=F32)


def _fold_lanes(x, op):
    out = x[:, :LANES]
    for j in range(1, x.shape[1] // LANES):
        out = op(out, x[:, j * LANES:(j + 1) * LANES])
    return out


def _ffn_body(*refs, d_ff, fc, final):
    if final:
        x_ref, g_ref, wgu_ref, wd_ref, gf_ref, o_ref = refs
    else:
        x_ref, g_ref, wgu_ref, wd_ref, o_ref = refs
    x = x_ref[...]
    h = _rms(x, g_ref[...]).astype(BF16)
    acc = jnp.zeros(x.shape, F32)
    for c in range(d_ff // fc):
        gg = _dot(h, wgu_ref[:, c * fc:(c + 1) * fc])
        uu = _dot(h, wgu_ref[:, d_ff + c * fc:d_ff + (c + 1) * fc])
        a = (gg * _sigmoid(gg) * uu).astype(BF16)
        acc = acc + _dot(a, wd_ref[c * fc:(c + 1) * fc, :])
    y = x + 0.5 * acc
    if final:
        y = _rms(y, gf_ref[...])
    o_ref[...] = y


def _ffn(x, g, wgu, wd, gf=None):
    t, d = x.shape
    d_ff = wd.shape[0]
    tm = 512 if t % 512 == 0 else 256
    fc = 256
    final = gf is not None
    const = lambda i: (0, 0)
    in_specs = [
        pl.BlockSpec((tm, d), lambda i: (i, 0)),
        pl.BlockSpec((1, d), const),
        pl.BlockSpec((d, 2 * d_ff), const, pipeline_mode=pl.Buffered(1)),
        pl.BlockSpec((d_ff, d), const, pipeline_mode=pl.Buffered(1)),
    ]
    args = [x, g.reshape(1, d), wgu, wd]
    if final:
        in_specs.append(pl.BlockSpec((1, d), const))
        args.append(gf.reshape(1, d))
    return pl.pallas_call(
        functools.partial(_ffn_body, d_ff=d_ff, fc=fc, final=final),
        grid=(t // tm,),
        in_specs=in_specs,
        out_specs=pl.BlockSpec((tm, d), lambda i: (i, 0)),
        out_shape=jax.ShapeDtypeStruct((t, d), F32),
        compiler_params=pltpu.CompilerParams(
            dimension_semantics=("arbitrary",),
            vmem_limit_bytes=_vmem_limit(48 * 1024 * 1024)),
        name="ffn_final" if final else "ffn",
    )(*args)


def _proj_body(x_ref, g_ref, wqkv_ref, wqi_ref, wkw_ref,
               q_ref, k_ref, v_ref, kb_ref, vb_ref, qi_ref, kw_ref):
    h = _rms(x_ref[...], g_ref[...]).astype(BF16)
    a = ATTN_WIDTH
    q_ref[...] = (_dot(h, wqkv_ref[:, :a]) * (HEAD_DIM ** -0.5)).astype(BF16)
    k = _dot(h, wqkv_ref[:, a:2 * a])
    k_ref[...] = k
    kb_ref[...] = k.astype(BF16)
    v = _dot(h, wqkv_ref[:, 2 * a:])
    v_ref[...] = v
    vb_ref[...] = v.astype(BF16)
    qi_ref[...] = (_dot(h, wqi_ref[...]) * (IDX_DIM ** -0.5)).astype(BF16)
    kw_ref[...] = _dot(h, wkw_ref[...])


def _attn_proj(x, g, wqkv, wqi, wkw):
    t, d = x.shape
    tm = 256
    a = ATTN_WIDTH
    nqi = IDX_HEADS * IDX_DIM
    const = lambda i: (0, 0)
    row = lambda i: (i, 0)
    return pl.pallas_call(
        _proj_body,
        grid=(t // tm,),
        in_specs=[
            pl.BlockSpec((tm, d), row),
            pl.BlockSpec((1, d), const),
            pl.BlockSpec((d, 3 * a), const, pipeline_mode=pl.Buffered(1)),
            pl.BlockSpec((d, nqi), const, pipeline_mode=pl.Buffered(1)),
            pl.BlockSpec((d, LANES), const, pipeline_mode=pl.Buffered(1)),
        ],
        out_specs=[
            pl.BlockSpec((tm, a), row), pl.BlockSpec((tm, a), row), pl.BlockSpec((tm, a), row),
            pl.BlockSpec((tm, a), row), pl.BlockSpec((tm, a), row),
            pl.BlockSpec((tm, nqi), row), pl.BlockSpec((tm, LANES), row),
        ],
        out_shape=[
            jax.ShapeDtypeStruct((t, a), BF16), jax.ShapeDtypeStruct((t, a), F32),
            jax.ShapeDtypeStruct((t, a), F32), jax.ShapeDtypeStruct((t, a), BF16),
            jax.ShapeDtypeStruct((t, a), BF16), jax.ShapeDtypeStruct((t, nqi), BF16),
            jax.ShapeDtypeStruct((t, LANES), F32),
        ],
        compiler_params=pltpu.CompilerParams(
            dimension_semantics=("arbitrary",),
            vmem_limit_bytes=_vmem_limit(40 * 1024 * 1024)),
        name="attn_proj",
    )(x, g.reshape(1, d), wqkv, wqi, wkw)


def _select_topk(count, smin, smax, n_adm, n_sel, n_cols):
    ksel = float(n_sel)
    c_hi = count(lambda s, col: s >= smax)
    few = n_adm <= ksel
    top_tied = jnp.logical_and(jnp.logical_not(few), c_hi >= ksel)
    lo0 = jnp.where(top_tied, smax, smin)
    done0 = jnp.where(jnp.logical_or(few, top_tied), 1.0, 0.0)

    def not_all_done(done):
        return (jnp.min(done) < 0.5).astype(jnp.int32)

    def cond(carry):
        return jnp.logical_and(carry[3] > 0, carry[4] < MAX_BISECT_ITERS)

    def body(carry):
        lo, hi, done, _, it = carry
        mid = lo * 0.5 + hi * 0.5
        conv = jnp.logical_or(mid <= lo, mid >= hi)
        c = count(lambda s, col: s >= mid)
        ge = c >= ksel
        upd = jnp.logical_and(done < 0.5, jnp.logical_not(conv))
        lo = jnp.where(jnp.logical_and(upd, ge), mid, lo)
        hi = jnp.where(jnp.logical_and(upd, jnp.logical_not(ge)), mid, hi)
        done = jnp.where(jnp.logical_or(conv, c == ksel), 1.0, done)
        return lo, hi, done, not_all_done(done), it + 1

    thr = lax.while_loop(cond, body, (lo0, smax, done0, not_all_done(done0), jnp.int32(0)))[0]

    c_ge = count(lambda s, col: s >= thr)
    has_tie = c_ge > ksel
    big = float(n_cols)

    def tie_break(_):
        c_gt = count(lambda s, col: s > thr)
        need = ksel - c_gt
        jlo = jnp.full_like(thr, -1.0)
        jhi = jnp.full_like(thr, big)

        def jbody(_, jc):
            jlo, jhi = jc
            jm = jnp.floor((jlo + jhi) * 0.5)
            c = count(lambda s, col: jnp.logical_and(s == thr, col <= jm))
            ok = c >= need
            return jnp.where(ok, jlo, jm), jnp.where(ok, jm, jhi)

        n_it = int(math.ceil(math.log2(n_cols + 2))) + 1
        _, jhi = lax.fori_loop(0, n_it, jbody, (jlo, jhi))
        return jnp.where(has_tie, jhi, big)

    any_tie = jnp.max(jnp.where(has_tie, 1.0, 0.0)) > 0.5
    jmax = lax.cond(any_tie, tie_break, lambda _: jnp.full_like(thr, big), 0)
    return thr, jmax


def _t5_bucket_np(rel):
    n = np.maximum(rel, 0)
    max_exact = N_BUCKETS // 2
    nf = np.maximum(n, max_exact).astype(np.float32)
    large = max_exact + (np.log(nf / np.float32(max_exact)) / np.float32(math.log(MAX_DISTANCE / max_exact))
                         * (N_BUCKETS - max_exact)).astype(np.int32)
    large = np.minimum(large, N_BUCKETS - 1)
    return np.where(n < max_exact, n, large)


_FAR_BUCKET = int(_t5_bucket_np(np.array([1 << 20]))[0])
assert int(_t5_bucket_np(np.array([Q_BLOCK + 1]))[0]) == _FAR_BUCKET


def _bias_rows(rel_bias, rel):
    bucket = _t5_bucket_np(rel).reshape(-1)
    onehot = (jnp.asarray(bucket)[None, :] == jnp.arange(N_BUCKETS, dtype=jnp.int32)[:, None]).astype(F32)
    centred = (rel_bias - rel_bias[_FAR_BUCKET][None, :]).T
    b = jnp.dot(centred, onehot, precision=lax.Precision.HIGHEST)
    return b.reshape((N_HEADS,) + rel.shape)


def _attn_prompt_body(qi_ref, kw_ref, kit_ref, q_ref, kt_ref, v_ref, toe_ref, x_ref, wo_ref,
                      o_ref, s_ref, lg_ref, oacc_ref, *, seq, n_sel):
    qb = pl.program_id(1)
    qn = Q_BLOCK
    sc = SCORE_CHUNK
    n_sc = (qb + sc // qn) // (sc // qn)

    qi = qi_ref[0, 0]
    wi = kw_ref[:, IDX_DIM:IDX_DIM + IDX_HEADS] * (IDX_HEADS ** -0.5)
    wcols = [jnp.broadcast_to(wi[:, h:h + 1], (qn, sc)) for h in range(IDX_HEADS)]
    qpos = lax.broadcasted_iota(jnp.int32, (qn, sc), 0) + qb * qn
    lane = lax.broadcasted_iota(jnp.int32, (qn, sc), 1)

    def score_chunk(c, carry):
        smin, smax = carry
        c0 = pl.multiple_of(c * sc, sc)
        s = _dot(qi, kit_ref[0, :, pl.ds(c0, sc)])
        tot = jnp.maximum(s[:qn], 0.0) * wcols[0]
        for h in range(1, IDX_HEADS):
            tot = tot + jnp.maximum(s[h * qn:(h + 1) * qn], 0.0) * wcols[h]
        adm = (lane + c0) <= qpos
        s_ref[:, pl.ds(c0, sc)] = jnp.where(adm, tot, NEG_INF)
        smin = jnp.minimum(smin, _fold_lanes(jnp.where(adm, tot, jnp.inf), jnp.minimum))
        smax = jnp.maximum(smax, _fold_lanes(jnp.where(adm, tot, NEG_INF), jnp.maximum))
        return smin, smax

    smin, smax = lax.fori_loop(
        0, n_sc, score_chunk,
        (jnp.full((qn, LANES), jnp.inf, F32), jnp.full((qn, LANES), NEG_INF, F32)))
    smin = jnp.min(smin, axis=1, keepdims=True)
    smax = jnp.max(smax, axis=1, keepdims=True)

    def count(pred):
        def body(c, acc):
            c0 = pl.multiple_of(c * sc, sc)
            s = s_ref[:, pl.ds(c0, sc)]
            col = (lane + c0).astype(F32)
            return acc + _fold_lanes(jnp.where(pred(s, col), 1.0, 0.0), jnp.add)
        acc = lax.fori_loop(0, n_sc, body, jnp.zeros((qn, LANES), F32))
        return jnp.sum(acc, axis=1, keepdims=True)

    n_adm = (lax.broadcasted_iota(jnp.int32, (qn, 1), 0) + qb * qn + 1).astype(F32)
    thr, jmax = _select_topk(count, smin, smax, n_adm, n_sel, seq)

    def mask_chunk(c, _):
        c0 = pl.multiple_of(c * sc, sc)
        s = s_ref[:, pl.ds(c0, sc)]
        col = (lane + c0).astype(F32)
        sel = jnp.logical_or(s > thr, jnp.logical_and(s == thr, col <= jmax))
        sel = jnp.logical_and(sel, s > NEG_INF)
        s_ref[:, pl.ds(c0, sc)] = jnp.where(sel, 0.0, NEG_INF)
        return 0

    lax.fori_loop(0, n_sc, mask_chunk, 0)

    n_far = jnp.maximum(qb - 1, 0)
    wc = WIDE_CHUNK
    n_wide = n_far // (wc // qn)
    tail0 = pl.multiple_of(n_wide * wc, wc)
    tw = TAIL_BLOCKS * qn
    near0 = pl.multiple_of(tail0 + jnp.where(qb > 0, n_far % (wc // qn), 0) * qn, qn)
    toe0 = pl.multiple_of(jnp.where(qb > 0, 0, qn), qn)
    s_ref[:, pl.ds(pl.multiple_of(n_sc * sc, sc), qn)] = jnp.full((qn, qn), NEG_INF, F32)
    lane_p = lax.broadcasted_iota(jnp.int32, (qn, LANES), 1)

    def pair_body(pr, _):
        p0 = pl.multiple_of(pr * LANES, LANES)
        qp = q_ref[:, pl.ds(p0, LANES)]
        zero = jnp.zeros_like(qp)
        qs = jnp.concatenate([jnp.where(lane_p < HEAD_DIM, qp, zero),
                              jnp.where(lane_p >= HEAD_DIM, qp, zero)], axis=0)

        def logits(c0, width):
            lg = _dot(qs, kt_ref[0, pl.ds(p0, LANES), pl.ds(c0, width)])
            m = s_ref[:, pl.ds(c0, width)]
            return lg + jnp.concatenate([m, m], axis=0)

        def p1(c0, width, mrun):
            lg = logits(c0, width)
            lg_ref[:, pl.ds(c0, width)] = lg
            return jnp.maximum(mrun, _fold_lanes(lg, jnp.maximum))

        def p2(c0, width, carry):
            lrun, acc = carry
            p = jnp.exp(lg_ref[:, pl.ds(c0, width)] - m)
            lrun = lrun + _fold_lanes(p, jnp.add)
            acc = acc + _dot(p.astype(BF16), v_ref[0, pl.ds(c0, width), pl.ds(p0, LANES)])
            return lrun, acc

        def over_wide(step, carry):
            done = 0
            for mult in (4, 2, 1):
                width = mult * wc
                trips = n_wide // 4 if mult == 4 else (n_wide // mult) % 2

                def body(i, c, width=width, base=done):
                    return step(pl.multiple_of(base + i * width, wc), width, c)

                carry = lax.fori_loop(0, trips, body, carry)
                done = done + trips * width
            return carry

        mrun = over_wide(p1, jnp.full((2 * qn, LANES), NEG_INF, F32))
        lg_ref[:, pl.ds(tail0, tw)] = logits(tail0, tw)
        lg_ref[:, pl.ds(near0, 2 * qn)] = lg_ref[:, pl.ds(near0, 2 * qn)] + toe_ref[pr, :, pl.ds(toe0, 2 * qn)]
        mrun = jnp.maximum(mrun, _fold_lanes(lg_ref[:, pl.ds(tail0, tw)], jnp.maximum))
        m = jnp.max(mrun, axis=1, keepdims=True)

        carry = over_wide(p2, (jnp.zeros((2 * qn, LANES), F32), jnp.zeros((2 * qn, LANES), F32)))
        lrun, acc = p2(tail0, tw, carry)
        o = acc / jnp.sum(lrun, axis=1, keepdims=True)
        oacc_ref[:, pl.ds(p0, LANES)] = jnp.where(lane_p < HEAD_DIM, o[:qn], o[qn:])
        return 0

    lax.fori_loop(0, N_PAIRS, pair_body, 0)

    o_ref[...] = x_ref[...] + _dot(oacc_ref[...].astype(BF16), wo_ref[...])


def _attn_prompt(x, q, kt, vb, qi4, kw, kit, toe, wo, batch, seq):
    d = x.shape[1]
    a = ATTN_WIDTH
    nqb = seq // Q_BLOCK
    n_sel = min(TOPK_MAX, seq // 4)
    blk = lambda b, i: (b * nqb + i, 0)
    per_b3 = lambda b, i: (b, 0, 0)
    seqp = kt.shape[2]
    assert seqp >= seq + (TAIL_BLOCKS - WIDE_CHUNK // Q_BLOCK) * Q_BLOCK
    resident = 2 * seqp * a * 2 + Q_BLOCK * seqp * 4 + 2 * Q_BLOCK * seqp * 4
    return pl.pallas_call(
        functools.partial(_attn_prompt_body, seq=seq, n_sel=n_sel),
        grid=(batch, nqb),
        in_specs=[
            pl.BlockSpec((1, 1, IDX_HEADS * Q_BLOCK, IDX_DIM), lambda b, i: (b, i, 0, 0)),
            pl.BlockSpec((Q_BLOCK, LANES), blk),
            pl.BlockSpec((1, IDX_DIM, seq), per_b3),
            pl.BlockSpec((Q_BLOCK, a), blk),
            pl.BlockSpec((1, a, seqp), per_b3, pipeline_mode=pl.Buffered(1)),
            pl.BlockSpec((1, seqp, a), per_b3, pipeline_mode=pl.Buffered(1)),
            pl.BlockSpec((N_PAIRS, 2 * Q_BLOCK, 3 * Q_BLOCK), lambda b, i: (0, 0, 0), pipeline_mode=pl.Buffered(1)),
            pl.BlockSpec((Q_BLOCK, d), blk),
            pl.BlockSpec((a, d), lambda b, i: (0, 0), pipeline_mode=pl.Buffered(1)),
        ],
        out_specs=pl.BlockSpec((Q_BLOCK, d), blk),
        out_shape=jax.ShapeDtypeStruct(x.shape, F32),
        scratch_shapes=[
            pltpu.VMEM((Q_BLOCK, seqp), F32),
            pltpu.VMEM((2 * Q_BLOCK, seqp), F32),
            pltpu.VMEM((Q_BLOCK, a), F32),
        ],
        compiler_params=pltpu.CompilerParams(
            dimension_semantics=("arbitrary", "arbitrary"),
            vmem_limit_bytes=_vmem_limit(resident + 14 * 1024 * 1024)),
        name="attn_prompt",
    )(qi4, kw, kit, q, kt, vb, toe, x, wo)


def _sample_scores_body(pt_ref, qi_ref, wb_ref, *refs, n_grp, dec_seq):
    page_refs = refs[:n_grp]
    kinew_ref, o_ref, onew_ref = refs[n_grp:]
    qi = qi_ref[0]
    wb = wb_ref[0]

    def score(keys_t, n_tiles):
        s = _dot(qi, keys_t.astype(BF16))
        t = jnp.maximum(s, 0.0) * jnp.concatenate([wb] * n_tiles, axis=1)
        tot = t[:dec_seq]
        for h in range(1, IDX_HEADS):
            tot = tot + t[h * dec_seq:(h + 1) * dec_seq]
        return tot

    o_ref[0] = score(jnp.concatenate([r[0] for r in page_refs], axis=1), n_grp)

    @pl.when(pl.program_id(1) == 0)
    def _():
        tot = score(kinew_ref[0], 1)
        j = lax.broadcasted_iota(jnp.int32, tot.shape, 1)
        tok = lax.broadcasted_iota(jnp.int32, tot.shape, 0)
        onew_ref[0] = jnp.where(j <= tok, tot, NEG_INF)


def _sample_scores(page_table, qi_s, wb, cache_kidx, ki_new_pad, dec_seq):
    n, n_pages = page_table.shape
    n_grp = SCORE_PAGES if n_pages % SCORE_PAGES == 0 else 1
    rows = IDX_HEADS * dec_seq
    per_seq = lambda i, g, pt: (i, 0, 0)

    def page(j):
        return lambda i, g, pt: (pt[i, g * n_grp + j], 0, 0)

    grid_spec = pltpu.PrefetchScalarGridSpec(
        num_scalar_prefetch=1,
        grid=(n, n_pages // n_grp),
        in_specs=[pl.BlockSpec((1, rows, IDX_DIM), per_seq), pl.BlockSpec((1, rows, LANES), per_seq)]
        + [pl.BlockSpec((1, IDX_DIM, PAGE_SIZE), page(j)) for j in range(n_grp)]
        + [pl.BlockSpec((1, IDX_DIM, PAGE_SIZE), per_seq)],
        out_specs=[pl.BlockSpec((1, dec_seq, n_grp * PAGE_SIZE), lambda i, g, pt: (i, 0, g)),
                   pl.BlockSpec((1, dec_seq, PAGE_SIZE), per_seq)],
    )
    return pl.pallas_call(
        functools.partial(_sample_scores_body, n_grp=n_grp, dec_seq=dec_seq),
        grid_spec=grid_spec,
        out_shape=[jax.ShapeDtypeStruct((n, dec_seq, n_pages * PAGE_SIZE), F32),
                   jax.ShapeDtypeStruct((n, dec_seq, PAGE_SIZE), F32)],
        compiler_params=pltpu.CompilerParams(dimension_semantics=("arbitrary", "arbitrary")),
        name="sample_scores",
    )(page_table, qi_s, wb, *([cache_kidx] * n_grp), ki_new_pad)


def _attn_sample_body(pt_ref, scp_ref, scn_ref, q_ref, *refs, n_pages, n_grp, dec_seq, n_sel):
    k_refs = refs[:n_grp]
    v_refs = refs[n_grp:2 * n_grp]
    (knew_ref, vnew_ref, bias_ref, x_ref, wo_ref,
     o_ref, mask_ref, q16_ref, m_ref, l_ref, acc_ref) = refs[2 * n_grp:]
    g = pl.program_id(1)
    n_groups = n_pages // n_grp
    n_cols = (n_pages + 1) * PAGE_SIZE
    ds = dec_seq

    @pl.when(g == 0)
    def _():
        s = jnp.concatenate([scp_ref[0], scn_ref[0]], axis=1)
        col = lax.broadcasted_iota(jnp.int32, s.shape, 1).astype(F32)
        fin = s > NEG_INF
        smin = jnp.min(jnp.where(fin, s, jnp.inf), axis=1, keepdims=True)
        smax = jnp.max(s, axis=1, keepdims=True)

        def count(pred):
            return jnp.sum(jnp.where(pred(s, col), 1.0, 0.0), axis=1, keepdims=True)

        tok = lax.broadcasted_iota(jnp.int32, (ds, 1), 0)
        n_adm = (tok + n_pages * PAGE_SIZE + 1).astype(F32)
        thr, jmax = _select_topk(count, smin, smax, n_adm, n_sel, n_cols)
        sel = jnp.logical_or(s > thr, jnp.logical_and(s == thr, col <= jmax))
        sel = jnp.logical_and(sel, fin)
        mask_ref[...] = jnp.where(sel, 0.0, NEG_INF)
        q = q_ref[0]
        for h in range(N_HEADS):
            qh = q[:, h * HEAD_DIM:(h + 1) * HEAD_DIM]
            q16_ref[h] = jnp.concatenate([qh, jnp.zeros_like(qh)], axis=0).astype(BF16)
        m_ref[...] = jnp.full(m_ref.shape, NEG_INF, F32)
        l_ref[...] = jnp.zeros(l_ref.shape, F32)
        acc_ref[...] = jnp.zeros(acc_ref.shape, F32)

    def head_t(page_refs, h):
        return jnp.concatenate([r[0, h] for r in page_refs], axis=1).astype(BF16)

    def pad_rows16(x):
        return jnp.concatenate([x, jnp.zeros_like(x)], axis=0).astype(BF16)

    def attend(k_refs, v_refs, mk, bias):
        lg = jnp.concatenate([_dot(q16_ref[h], head_t(k_refs, h))[:ds] for h in range(N_HEADS)], axis=0)
        lg = lg + jnp.concatenate([mk] * N_HEADS, axis=0) + bias
        m_old = m_ref[...]
        m_new = jnp.maximum(m_old, jnp.max(lg, axis=1, keepdims=True))
        m_safe = jnp.where(m_new == NEG_INF, 0.0, m_new)
        alpha = jnp.exp(m_old - m_safe)
        pr = jnp.exp(lg - m_safe)
        l_ref[...] = alpha * l_ref[...] + jnp.sum(pr, axis=1, keepdims=True)
        pv = jnp.concatenate([_dot_nt(pad_rows16(pr[h * ds:(h + 1) * ds]), head_t(v_refs, h))[:ds]
                              for h in range(N_HEADS)], axis=0)
        acc_ref[...] = alpha * acc_ref[...] + pv
        m_ref[...] = m_new

    @pl.when(g < n_groups)
    def _():
        c0 = pl.multiple_of(g * (n_grp * PAGE_SIZE), n_grp * PAGE_SIZE)
        mk = mask_ref[:, pl.ds(c0, n_grp * PAGE_SIZE)]
        near = jnp.where(g == n_groups - 1, bias_ref[1], 0.0)
        if n_grp > 1:
            near = jnp.concatenate([jnp.zeros((near.shape[0], (n_grp - 1) * PAGE_SIZE), F32), near], axis=1)
        attend(k_refs, v_refs, mk, near)

    @pl.when(g == n_groups)
    def _():
        attend([knew_ref], [vnew_ref], mask_ref[:, pl.ds(n_pages * PAGE_SIZE, PAGE_SIZE)], bias_ref[2])
        o = acc_ref[...] / l_ref[...]
        out = jnp.zeros((2 * ds, wo_ref.shape[1]), F32)
        for h in range(N_HEADS):
            out = out + _dot(pad_rows16(o[h * ds:(h + 1) * ds]), wo_ref[h * HEAD_DIM:(h + 1) * HEAD_DIM, :])
        o_ref[0] = x_ref[0] + out[:ds]


def _attn_sample(page_table, scores_past, scores_new, q_s, cache_k, cache_v, k_new, v_new, bias3, x_s, wo, dec_seq):
    n, n_pages = page_table.shape
    n_grp = ATTN_PAGES if n_pages % ATTN_PAGES == 0 else 1
    n_groups = n_pages // n_grp
    a = ATTN_WIDTH
    d = x_s.shape[-1]
    rows = N_HEADS * dec_seq
    page_shape = (1, N_HEADS, HEAD_DIM, PAGE_SIZE)
    n_cols = (n_pages + 1) * PAGE_SIZE
    n_sel = min(TOPK_MAX, (n_pages * PAGE_SIZE + dec_seq) // 4)
    per_seq = lambda i, g, pt: (i, 0, 0)
    per_seq4 = lambda i, g, pt: (i, 0, 0, 0)

    def page(j):
        return lambda i, g, pt: (pt[i, jnp.minimum(g, n_groups - 1) * n_grp + j], 0, 0, 0)

    page_specs = [pl.BlockSpec(page_shape, page(j)) for j in range(n_grp)]
    grid_spec = pltpu.PrefetchScalarGridSpec(
        num_scalar_prefetch=1,
        grid=(n, n_groups + 1),
        in_specs=[
            pl.BlockSpec((1, dec_seq, n_pages * PAGE_SIZE), per_seq),
            pl.BlockSpec((1, dec_seq, PAGE_SIZE), per_seq),
            pl.BlockSpec((1, dec_seq, a), per_seq),
        ] + page_specs + page_specs + [
            pl.BlockSpec(page_shape, per_seq4),
            pl.BlockSpec(page_shape, per_seq4),
            pl.BlockSpec((3, rows, PAGE_SIZE), lambda i, g, pt: (0, 0, 0)),
            pl.BlockSpec((1, dec_seq, d), per_seq),
            pl.BlockSpec((a, d), lambda i, g, pt: (0, 0)),
        ],
        out_specs=pl.BlockSpec((1, dec_seq, d), per_seq),
        scratch_shapes=[
            pltpu.VMEM((dec_seq, n_cols), F32),
            pltpu.VMEM((N_HEADS, 2 * dec_seq, HEAD_DIM), BF16),
            pltpu.VMEM((rows, 1), F32),
            pltpu.VMEM((rows, 1), F32),
            pltpu.VMEM((rows, HEAD_DIM), F32),
        ],
    )
    return pl.pallas_call(
        functools.partial(_attn_sample_body, n_pages=n_pages, n_grp=n_grp, dec_seq=dec_seq, n_sel=n_sel),
        grid_spec=grid_spec,
        out_shape=jax.ShapeDtypeStruct(x_s.shape, F32),
        compiler_params=pltpu.CompilerParams(
            dimension_semantics=("arbitrary", "arbitrary"),
            vmem_limit_bytes=_vmem_limit(44 * 1024 * 1024)),
        name="attn_sample",
    )(page_table, scores_past, scores_new, q_s, *([cache_k] * n_grp), *([cache_v] * n_grp),
      k_new, v_new, bias3, x_s, wo)


def _gelu_tanh(x):
    return 0.5 * x * (1.0 + jnp.tanh(math.sqrt(2.0 / math.pi) * (x + 0.044715 * (x * x * x))))


def _lru_body(*refs, sample, tm, width):
    if sample:
        (x_ref, g_ref, win_ref, cw_ref, cb_ref, wax_ref, bax_ref, lam_ref, wout_ref,
         r1_ref, r2_ref, r3_ref, h0_ref, o_ref, xr_ref, hs_ref, a_ref, u_ref) = refs
    else:
        (x_ref, g_ref, win_ref, cw_ref, cb_ref, wax_ref, bax_ref, lam_ref, wout_ref,
         o_ref, xr_ref, hs_ref, a_ref, u_ref, prev_ref, hc_ref) = refs

        @pl.when(pl.program_id(1) == 0)
        def _():
            prev_ref[...] = jnp.zeros(prev_ref.shape, F32)
            hc_ref[...] = jnp.zeros(hc_ref.shape, F32)

    w = width
    x = x_ref[...]
    h = _rms(x, g_ref[...]).astype(BF16)
    z = _dot(h, win_ref[...])
    gate = _gelu_tanh(z[:, :w])
    xr = z[:, w:]
    row = lax.broadcasted_iota(jnp.int32, (tm, w), 0)
    pos8 = row & (SUBLANES - 1)

    shifted = []
    for k in range(1, CONV_WIDTH):
        rolled = pltpu.roll(xr, k, 0)
        if sample:
            first = (r1_ref, r2_ref, r3_ref)[k - 1][...]
            shifted.append(jnp.where(pos8 < k, first, rolled))
        else:
            first = jnp.concatenate([pltpu.roll(prev_ref[...], k, 0)] * (tm // SUBLANES), axis=0)
            shifted.append(jnp.where(row < k, first, rolled))
    xc = cb_ref[...]
    for j in range(CONV_WIDTH - 1):
        xc = xc + shifted[CONV_WIDTH - 2 - j] * cw_ref[j:j + 1, :]
    xc = xc + xr * cw_ref[CONV_WIDTH - 1:CONV_WIDTH, :]

    lam = lam_ref[...]
    nl = -lam
    softplus = jnp.maximum(nl, 0.0) + jnp.log1p(jnp.exp(-jnp.abs(nl)))
    xcb = xc.astype(BF16)
    bw = w // LRU_BLOCKS
    for gi in range(LRU_BLOCKS):
        sl = slice(gi * bw, (gi + 1) * bw)
        ga = _dot(xcb[:, sl], wax_ref[gi]) + bax_ref[gi]
        r = _sigmoid(ga[:, :bw])
        ig = _sigmoid(ga[:, bw:])
        log_a = -LRU_C * r * softplus[:, sl]
        a_ref[:, sl] = jnp.exp(log_a)
        th = jnp.tanh(log_a)
        u_ref[:, sl] = jnp.sqrt(-2.0 * th / (1.0 - th)) * (ig * xc[:, sl])

    a_c = a_ref[...]
    u_c = u_ref[...]
    for s in (1, 2, 4):
        ok = pos8 >= s
        a_sh = jnp.where(ok, pltpu.roll(a_c, s, 0), 1.0)
        u_sh = jnp.where(ok, pltpu.roll(u_c, s, 0), 0.0)
        u_c = u_c + a_c * u_sh
        a_c = a_c * a_sh
    if sample:
        hs = u_c + a_c * h0_ref[...]
    else:
        carry = hc_ref[...]
        parts = []
        for gi in range(tm // SUBLANES):
            sl = slice(gi * SUBLANES, (gi + 1) * SUBLANES)
            hg = u_c[sl] + a_c[sl] * carry
            parts.append(hg)
            carry = jnp.broadcast_to(hg[SUBLANES - 1:SUBLANES], (SUBLANES, w))
        hs = jnp.concatenate(parts, axis=0)
        hc_ref[...] = carry
        prev_ref[...] = xr[tm - SUBLANES:]

    y = (hs * gate).astype(BF16)
    o_ref[...] = x + _dot(y, wout_ref[...])
    if sample:
        xr_ref[...] = xr
        hs_ref[...] = hs
    else:
        xr_ref[0] = xr[tm - SUBLANES:]
        hs_ref[0] = hs[tm - SUBLANES:]


def _lru(x, g, win, cw, cb, wax, bax, lam, wout, batch=None, seq=None, sample_state=None):
    t, d = x.shape
    w = wout.shape[0]
    bw = w // LRU_BLOCKS
    sample = sample_state is not None
    const2 = lambda *_: (0, 0)
    const3 = lambda *_: (0, 0, 0)
    single = pl.Buffered(1)
    if sample:
        tm = t
        grid = (1,)
        row = lambda i: (0, 0)
        dims = ("arbitrary",)
    else:
        tm = 256
        nt = seq // tm
        grid = (batch, nt)
        row = lambda b, i: (b * nt + i, 0)
        dims = ("arbitrary", "arbitrary")
    in_specs = [
        pl.BlockSpec((tm, d), row),
        pl.BlockSpec((1, d), const2),
        pl.BlockSpec((d, 2 * w), const2, pipeline_mode=single),
        pl.BlockSpec((CONV_WIDTH, w), const2),
        pl.BlockSpec((1, w), const2),
        pl.BlockSpec((LRU_BLOCKS, bw, 2 * bw), const3),
        pl.BlockSpec((LRU_BLOCKS, 1, 2 * bw), const3),
        pl.BlockSpec((1, w), const2),
        pl.BlockSpec((w, d), const2, pipeline_mode=single),
    ]
    args = [x, g.reshape(1, d), win, cw, cb.reshape(1, w), wax, bax, lam.reshape(1, w), wout]
    scratch = [pltpu.VMEM((tm, w), F32), pltpu.VMEM((tm, w), F32)]
    if sample:
        in_specs += [pl.BlockSpec((tm, w), row)] * 4
        args += list(sample_state)
        out_specs = [pl.BlockSpec((tm, d), row), pl.BlockSpec((tm, w), row), pl.BlockSpec((tm, w), row)]
        out_shape = [jax.ShapeDtypeStruct((t, d), F32), jax.ShapeDtypeStruct((t, w), F32),
                     jax.ShapeDtypeStruct((t, w), F32)]
    else:
        tail = lambda b, i: (b, 0, 0)
        out_specs = [pl.BlockSpec((tm, d), row), pl.BlockSpec((1, SUBLANES, w), tail),
                     pl.BlockSpec((1, SUBLANES, w), tail)]
        out_shape = [jax.ShapeDtypeStruct((t, d), F32), jax.ShapeDtypeStruct((batch, SUBLANES, w), F32),
                     jax.ShapeDtypeStruct((batch, SUBLANES, w), F32)]
        scratch += [pltpu.VMEM((SUBLANES, w), F32), pltpu.VMEM((SUBLANES, w), F32)]
    return pl.pallas_call(
        functools.partial(_lru_body, sample=sample, tm=tm, width=w),
        grid=grid,
        in_specs=in_specs,
        out_specs=out_specs,
        out_shape=out_shape,
        scratch_shapes=scratch,
        compiler_params=pltpu.CompilerParams(
            dimension_semantics=dims,
            vmem_limit_bytes=_vmem_limit(40 * 1024 * 1024)),
        name="lru_sample" if sample else "lru_prompt",
    )(*args)


def kernel(x_prompt, x_sample, cache_k, cache_v, cache_kidx, page_table, state_conv, state_h, rel_bias, attn_w_in, attn_w_out, lru_w_in, lru_conv_w, lru_conv_b, lru_w_a, lru_b_a, lru_w_x, lru_b_x, lru_lambda, lru_w_out, norm_g, ffn_w_gu, ffn_w_down, final_norm_g):
    batch, seq, d = x_prompt.shape
    n_dec, dec_seq, _ = x_sample.shape
    n_pages = page_table.shape[1]
    n_pool = cache_k.shape[0]
    a = ATTN_WIDTH
    w = lru_w_out.shape[0]
    assert seq % SCORE_CHUNK == 0 and dec_seq == SUBLANES and page_table.shape[1] >= 2

    xp = x_prompt.reshape(batch * seq, d)
    xs = x_sample.reshape(n_dec * dec_seq, d)

    wgu = ffn_w_gu.astype(BF16)
    wdn = ffn_w_down.astype(BF16)
    qi_end = 3 * a + IDX_HEADS * IDX_DIM
    wqkv = attn_w_in[:, :3 * a].astype(BF16)
    wqi = attn_w_in[:, 3 * a:qi_end].astype(BF16)
    wkw = jnp.pad(attn_w_in[:, qi_end:], ((0, 0), (0, LANES - (IDX_DIM + IDX_HEADS)))).astype(BF16)
    wo = attn_w_out.astype(BF16)
    lwin = lru_w_in.astype(BF16)
    lwout = lru_w_out.astype(BF16)
    wax = jnp.concatenate([lru_w_a, lru_w_x], axis=-1).astype(BF16)
    bax = jnp.concatenate([lru_b_a, lru_b_x], axis=-1)[:, None, :]

    xp = _ffn(xp, norm_g[0, 0], wgu[0, 0], wdn[0, 0])
    xs = _ffn(xs, norm_g[0, 0], wgu[0, 0], wdn[0, 0])

    q_p, k_p, v_p, kb_p, vb_p, qi_p, kw_p = _attn_proj(xp, norm_g[0, 1], wqkv, wqi, wkw)
    q_s, k_s, v_s, _, _, qi_s, kw_s = _attn_proj(xs, norm_g[0, 1], wqkv, wqi, wkw)

    nqb = seq // Q_BLOCK
    qi4 = qi_p.reshape(batch, nqb, Q_BLOCK, IDX_HEADS, IDX_DIM).transpose(0, 1, 3, 2, 4)
    qi4 = qi4.reshape(batch, nqb, IDX_HEADS * Q_BLOCK, IDX_DIM)
    kit = kw_p[:, :IDX_DIM].astype(BF16).reshape(batch, seq, IDX_DIM).transpose(0, 2, 1)
    key_pad = ((0, 0), (0, Q_BLOCK), (0, 0))
    kt = jnp.pad(kb_p.reshape(batch, seq, a), key_pad).transpose(0, 2, 1)
    vb3 = jnp.pad(vb_p.reshape(batch, seq, a), key_pad)
    i_q = np.arange(Q_BLOCK)[:, None]
    c_k = np.arange(3 * Q_BLOCK)[None, :]
    toe = _bias_rows(rel_bias, Q_BLOCK + i_q - c_k)
    toe = toe.reshape(N_PAIRS, 2 * Q_BLOCK, 3 * Q_BLOCK)
    xp = _attn_prompt(xp, q_p, kt, vb3, qi4, kw_p, kit, toe, wo, batch, seq)

    qi_s3 = qi_s.reshape(n_dec, dec_seq, IDX_HEADS, IDX_DIM).transpose(0, 2, 1, 3)
    qi_s3 = qi_s3.reshape(n_dec, IDX_HEADS * dec_seq, IDX_DIM)
    wi_s = kw_s[:, IDX_DIM:IDX_DIM + IDX_HEADS].reshape(n_dec, dec_seq, IDX_HEADS) * (IDX_HEADS ** -0.5)
    wb = jnp.broadcast_to(wi_s.transpose(0, 2, 1).reshape(n_dec, IDX_HEADS * dec_seq, 1),
                          (n_dec, IDX_HEADS * dec_seq, LANES))
    new_keys = PAGE_SIZE - dec_seq
    ki_new = jnp.pad(kw_s[:, :IDX_DIM].reshape(n_dec, dec_seq, IDX_DIM).transpose(0, 2, 1),
                     ((0, 0), (0, 0), (0, new_keys)))
    scores_past, scores_new = _sample_scores(page_table, qi_s3, wb, cache_kidx.transpose(0, 2, 1), ki_new, dec_seq)
    new_pad = ((0, 0), (0, 0), (0, 0), (0, new_keys))
    k_new = jnp.pad(k_s.reshape(n_dec, dec_seq, N_HEADS, HEAD_DIM).transpose(0, 2, 3, 1), new_pad)
    v_new = jnp.pad(v_s.reshape(n_dec, dec_seq, N_HEADS, HEAD_DIM).transpose(0, 2, 3, 1), new_pad)
    tok = np.arange(dec_seq)[:, None]
    jj = np.arange(PAGE_SIZE)[None, :]
    b_last = _bias_rows(rel_bias, PAGE_SIZE + tok - jj)
    b_new = _bias_rows(rel_bias, tok - jj)
    bias3 = jnp.stack([jnp.zeros_like(b_last), b_last, b_new]).reshape(3, N_HEADS * dec_seq, PAGE_SIZE)
    xs3 = _attn_sample(page_table, scores_past, scores_new, q_s.astype(F32).reshape(n_dec, dec_seq, a),
                       cache_k.transpose(0, 2, 3, 1), cache_v.transpose(0, 2, 3, 1), k_new, v_new, bias3, xs.reshape(n_dec, dec_seq, d), wo, dec_seq)
    xs = xs3.reshape(n_dec * dec_seq, d)

    xp = _ffn(xp, norm_g[0, 2], wgu[0, 1], wdn[0, 1])
    xs = _ffn(xs, norm_g[0, 2], wgu[0, 1], wdn[0, 1])

    xp = _ffn(xp, norm_g[1, 0], wgu[1, 0], wdn[1, 0])
    xs = _ffn(xs, norm_g[1, 0], wgu[1, 0], wdn[1, 0])

    lru_args = (lwin, lru_conv_w, lru_conv_b, wax, bax, lru_lambda, lwout)
    xp, conv_p8, h_p8 = _lru(xp, norm_g[1, 1], *lru_args, batch=batch, seq=seq)
    firsts = []
    for k in range(1, CONV_WIDTH):
        r = jnp.pad(state_conv[:, CONV_WIDTH - 1 - k:, :], ((0, 0), (0, dec_seq - k), (0, 0)))
        firsts.append(r.reshape(n_dec * dec_seq, w))
    h0 = jnp.broadcast_to(state_h[:, None, :], (n_dec, dec_seq, w)).reshape(n_dec * dec_seq, w)
    xs, xr_s, hs_s = _lru(xs, norm_g[1, 1], *lru_args, sample_state=firsts + [h0])

    xp = _ffn(xp, norm_g[1, 2], wgu[1, 1], wdn[1, 1], gf=final_norm_g)
    xs = _ffn(xs, norm_g[1, 2], wgu[1, 1], wdn[1, 1], gf=final_norm_g)

    nh, hd = N_HEADS, HEAD_DIM
    keep = CONV_WIDTH - 1
    return (
        xp.reshape(batch, seq, d),
        xs.reshape(n_dec, dec_seq, d),
        k_p.reshape(batch, seq, nh, hd),
        v_p.reshape(batch, seq, nh, hd),
        kw_p[:, :IDX_DIM].reshape(batch, seq, IDX_DIM),
        k_s.reshape(n_dec, dec_seq, nh, hd),
        v_s.reshape(n_dec, dec_seq, nh, hd),
        kw_s[:, :IDX_DIM].reshape(n_dec, dec_seq, IDX_DIM),
        conv_p8[:, SUBLANES - keep:, :],
        h_p8[:, SUBLANES - 1, :],
        xr_s.reshape(n_dec, dec_seq, w)[:, dec_seq - keep:, :],
        hs_s.reshape(n_dec, dec_seq, w)[:, dec_seq - 1, :],
    )
```

```python
import functools
import math

import numpy as np
import jax
import jax.numpy as jnp
from jax import lax
from jax.experimental import pallas as pl
from jax.experimental.pallas import tpu as pltpu

F32 = jnp.float32
BF16 = jnp.bfloat16
NEG_INF = float("-inf")

N_HEADS = 16
HEAD_DIM = 64
ATTN_WIDTH = N_HEADS * HEAD_DIM
IDX_HEADS = 8
IDX_DIM = 64
TOPK_MAX = 256
Q_BLOCK = 128
PAGE_SIZE = 128
N_BUCKETS = 32
MAX_DISTANCE = 128
LRU_BLOCKS = 8
CONV_WIDTH = 4
LRU_C = 8.0
RMS_EPS = 1e-6
LOG2E = math.log2(math.e)

V7X_VMEM_BYTES = 64 * 1024 * 1024
LANES = 128
SUBLANES = 8

N_PAIRS = N_HEADS // 2
SCORE_CHUNK = 512
WIDE_CHUNK = 512
SCORE_PAGES = 8
ATTN_PAGES = 8
TAIL_BLOCKS = 5
MAX_BISECT_ITERS = 320


def _vmem_limit(nbytes):
    return int(min(nbytes, V7X_VMEM_BYTES - 4 * 1024 * 1024))


def _rms(x, g):
    ms = jnp.mean(x * x, axis=-1, keepdims=True)
    return x * lax.rsqrt(ms + RMS_EPS) * g


def _sigmoid(x):
    return 1.0 / (1.0 + jnp.exp(-x))


def _dot(a, b):
    return jnp.dot(a, b, preferred_element_type=F32)


def _dot_nt(a, b):
    return lax.dot_general(a, b, (((1,), (1,)), ((), ())), preferred_element_type=F32)


def _fold_lanes(x, op):
    out = x[:, :LANES]
    for j in range(1, x.shape[1] // LANES):
        out = op(out, x[:, j * LANES:(j + 1) * LANES])
    return out


def _ffn_body(*refs, d_ff, fc, final):
    if final:
        x_ref, g_ref, wgu_ref, wd_ref, gf_ref, o_ref = refs
    else:
        x_ref, g_ref, wgu_ref, wd_ref, o_ref = refs
    x = x_ref[...]
    h = _rms(x, g_ref[...]).astype(BF16)
    acc = jnp.zeros(x.shape, F32)
    for c in range(d_ff // fc):
        gg = _dot(h, wgu_ref[:, c * fc:(c + 1) * fc])
        uu = _dot(h, wgu_ref[:, d_ff + c * fc:d_ff + (c + 1) * fc])
        a = (gg * _sigmoid(gg) * uu).astype(BF16)
        acc = acc + _dot(a, wd_ref[c * fc:(c + 1) * fc, :])
    y = x + 0.5 * acc
    if final:
        y = _rms(y, gf_ref[...])
    o_ref[...] = y


def _ffn(x, g, wgu, wd, gf=None):
    t, d = x.shape
    d_ff = wd.shape[0]
    tm = 512 if t % 512 == 0 else 256
    fc = 256
    final = gf is not None
    const = lambda i: (0, 0)
    in_specs = [
        pl.BlockSpec((tm, d), lambda i: (i, 0)),
        pl.BlockSpec((1, d), const),
        pl.BlockSpec((d, 2 * d_ff), const, pipeline_mode=pl.Buffered(1)),
        pl.BlockSpec((d_ff, d), const, pipeline_mode=pl.Buffered(1)),
    ]
    args = [x, g.reshape(1, d), wgu, wd]
    if final:
        in_specs.append(pl.BlockSpec((1, d), const))
        args.append(gf.reshape(1, d))
    return pl.pallas_call(
        functools.partial(_ffn_body, d_ff=d_ff, fc=fc, final=final),
        grid=(t // tm,),
        in_specs=in_specs,
        out_specs=pl.BlockSpec((tm, d), lambda i: (i, 0)),
        out_shape=jax.ShapeDtypeStruct((t, d), F32),
        compiler_params=pltpu.CompilerParams(
            dimension_semantics=("arbitrary",),
            vmem_limit_bytes=_vmem_limit(48 * 1024 * 1024)),
        name="ffn_final" if final else "ffn",
    )(*args)


def _proj_body(x_ref, g_ref, wqkv_ref, wqi_ref, wkw_ref,
               q_ref, k_ref, v_ref, kb_ref, vb_ref, qi_ref, kw_ref, *maybe_kwt_ref, q_scale, key_minor):
    h = _rms(x_ref[...], g_ref[...]).astype(BF16)
    a = ATTN_WIDTH
    q_ref[...] = (_dot(h, wqkv_ref[:, :a]) * q_scale).astype(BF16)
    k = _dot(h, wqkv_ref[:, a:2 * a])
    v = _dot(h, wqkv_ref[:, 2 * a:])
    kw = _dot(h, wkw_ref[...])
    if key_minor:
        kt = k.T
        k_ref[0] = kt
        kb_ref[0] = kt.astype(BF16)
        v_ref[0] = v.T
        maybe_kwt_ref[0][0] = kw.T
    else:
        k_ref[...] = k
        kb_ref[...] = k.astype(BF16)
        v_ref[...] = v
    vb_ref[...] = v.astype(BF16)
    qi_ref[...] = (_dot(h, wqi_ref[...]) * (IDX_DIM ** -0.5)).astype(BF16)
    kw_ref[...] = kw


def _attn_proj(x, g, wqkv, wqi, wkw, q_scale, batch=None):
    t, d = x.shape
    tm = 256
    a = ATTN_WIDTH
    nqi = IDX_HEADS * IDX_DIM
    const = lambda i: (0, 0)
    row = lambda i: (i, 0)
    key_minor = batch is not None
    if key_minor:
        seq = t // batch
        nt = seq // tm
        chan = lambda i: (i // nt, 0, i % nt)
        kv_spec = pl.BlockSpec((1, a, tm), chan)
        kv_f32 = jax.ShapeDtypeStruct((batch, a, seq), F32)
        kv_bf16 = jax.ShapeDtypeStruct((batch, a, seq), BF16)
    else:
        kv_spec = pl.BlockSpec((tm, a), row)
        kv_f32 = jax.ShapeDtypeStruct((t, a), F32)
        kv_bf16 = jax.ShapeDtypeStruct((t, a), BF16)
    out_specs = [
        pl.BlockSpec((tm, a), row), kv_spec, kv_spec, kv_spec, pl.BlockSpec((tm, a), row),
        pl.BlockSpec((tm, nqi), row), pl.BlockSpec((tm, LANES), row),
    ]
    out_shape = [
        jax.ShapeDtypeStruct((t, a), BF16), kv_f32, kv_f32, kv_bf16, jax.ShapeDtypeStruct((t, a), BF16),
        jax.ShapeDtypeStruct((t, nqi), BF16), jax.ShapeDtypeStruct((t, LANES), F32),
    ]
    if key_minor:
        out_specs.append(pl.BlockSpec((1, LANES, tm), chan))
        out_shape.append(jax.ShapeDtypeStruct((batch, LANES, seq), F32))
    return pl.pallas_call(
        functools.partial(_proj_body, q_scale=q_scale, key_minor=key_minor),
        grid=(t // tm,),
        in_specs=[
            pl.BlockSpec((tm, d), row),
            pl.BlockSpec((1, d), const),
            pl.BlockSpec((d, 3 * a), const, pipeline_mode=pl.Buffered(1)),
            pl.BlockSpec((d, nqi), const, pipeline_mode=pl.Buffered(1)),
            pl.BlockSpec((d, LANES), const, pipeline_mode=pl.Buffered(1)),
        ],
        out_specs=out_specs,
        out_shape=out_shape,
        compiler_params=pltpu.CompilerParams(
            dimension_semantics=("arbitrary",),
            vmem_limit_bytes=_vmem_limit(44 * 1024 * 1024)),
        name="attn_proj",
    )(x, g.reshape(1, d), wqkv, wqi, wkw)


def _select_topk(count, smin, smax, n_adm, n_sel, n_cols):
    ksel = float(n_sel)
    c_hi = count(lambda s, col: s >= smax)
    few = n_adm <= ksel
    top_tied = jnp.logical_and(jnp.logical_not(few), c_hi >= ksel)
    lo0 = jnp.where(top_tied, smax, smin)
    done0 = jnp.where(jnp.logical_or(few, top_tied), 1.0, 0.0)

    def not_all_done(done):
        return (jnp.min(done) < 0.5).astype(jnp.int32)

    def cond(carry):
        return jnp.logical_and(carry[3] > 0, carry[4] < MAX_BISECT_ITERS)

    def body(carry):
        lo, hi, done, _, it = carry
        mid = lo * 0.5 + hi * 0.5
        conv = jnp.logical_or(mid <= lo, mid >= hi)
        c = count(lambda s, col: s >= mid)
        ge = c >= ksel
        upd = jnp.logical_and(done < 0.5, jnp.logical_not(conv))
        lo = jnp.where(jnp.logical_and(upd, ge), mid, lo)
        hi = jnp.where(jnp.logical_and(upd, jnp.logical_not(ge)), mid, hi)
        done = jnp.where(jnp.logical_or(conv, c == ksel), 1.0, done)
        return lo, hi, done, not_all_done(done), it + 1

    thr = lax.while_loop(cond, body, (lo0, smax, done0, not_all_done(done0), jnp.int32(0)))[0]

    c_ge = count(lambda s, col: s >= thr)
    has_tie = c_ge > ksel
    big = float(n_cols)

    def tie_break(_):
        c_gt = count(lambda s, col: s > thr)
        need = ksel - c_gt
        jlo = jnp.full_like(thr, -1.0)
        jhi = jnp.full_like(thr, big)

        def jbody(_, jc):
            jlo, jhi = jc
            jm = jnp.floor((jlo + jhi) * 0.5)
            c = count(lambda s, col: jnp.logical_and(s == thr, col <= jm))
            ok = c >= need
            return jnp.where(ok, jlo, jm), jnp.where(ok, jm, jhi)

        n_it = int(math.ceil(math.log2(n_cols + 2))) + 1
        _, jhi = lax.fori_loop(0, n_it, jbody, (jlo, jhi))
        return jnp.where(has_tie, jhi, big)

    any_tie = jnp.max(jnp.where(has_tie, 1.0, 0.0)) > 0.5
    jmax = lax.cond(any_tie, tie_break, lambda _: jnp.full_like(thr, big), 0)
    return thr, jmax


def _t5_bucket_np(rel):
    n = np.maximum(rel, 0)
    max_exact = N_BUCKETS // 2
    nf = np.maximum(n, max_exact).astype(np.float32)
    large = max_exact + (np.log(nf / np.float32(max_exact)) / np.float32(math.log(MAX_DISTANCE / max_exact))
                         * (N_BUCKETS - max_exact)).astype(np.int32)
    large = np.minimum(large, N_BUCKETS - 1)
    return np.where(n < max_exact, n, large)


_FAR_BUCKET = int(_t5_bucket_np(np.array([1 << 20]))[0])
assert int(_t5_bucket_np(np.array([Q_BLOCK + 1]))[0]) == _FAR_BUCKET


def _bias_rows(rel_bias, rel):
    bucket = _t5_bucket_np(rel).reshape(-1)
    onehot = (jnp.asarray(bucket)[None, :] == jnp.arange(N_BUCKETS, dtype=jnp.int32)[:, None]).astype(F32)
    centred = (rel_bias - rel_bias[_FAR_BUCKET][None, :]).T
    b = jnp.dot(centred, onehot, precision=lax.Precision.HIGHEST)
    return b.reshape((N_HEADS,) + rel.shape)


def _attn_prompt_body(qi_ref, kw_ref, kit_ref, q_ref, kt_ref, v_ref, toe_ref, x_ref, wo_ref,
                      o_ref, s_ref, lg_ref, oacc_ref, *, seq, n_sel):
    qb = pl.program_id(1)
    qn = Q_BLOCK
    sc = SCORE_CHUNK
    n_sc = (qb + sc // qn) // (sc // qn)

    qi = qi_ref[0, 0]
    wi = kw_ref[:, IDX_DIM:IDX_DIM + IDX_HEADS] * (IDX_HEADS ** -0.5)
    wcols = [jnp.broadcast_to(wi[:, h:h + 1], (qn, sc)) for h in range(IDX_HEADS)]
    qpos = lax.broadcasted_iota(jnp.int32, (qn, sc), 0) + qb * qn
    lane = lax.broadcasted_iota(jnp.int32, (qn, sc), 1)

    def score_chunk(c, carry):
        smin, smax = carry
        c0 = pl.multiple_of(c * sc, sc)
        s = _dot(qi, kit_ref[0, :, pl.ds(c0, sc)])
        tot = jnp.maximum(s[:qn], 0.0) * wcols[0]
        for h in range(1, IDX_HEADS):
            tot = tot + jnp.maximum(s[h * qn:(h + 1) * qn], 0.0) * wcols[h]
        adm = (lane + c0) <= qpos
        s_ref[:, pl.ds(c0, sc)] = jnp.where(adm, tot, NEG_INF)
        smin = jnp.minimum(smin, _fold_lanes(jnp.where(adm, tot, jnp.inf), jnp.minimum))
        smax = jnp.maximum(smax, _fold_lanes(jnp.where(adm, tot, NEG_INF), jnp.maximum))
        return smin, smax

    smin, smax = lax.fori_loop(
        0, n_sc, score_chunk,
        (jnp.full((qn, LANES), jnp.inf, F32), jnp.full((qn, LANES), NEG_INF, F32)))
    smin = jnp.min(smin, axis=1, keepdims=True)
    smax = jnp.max(smax, axis=1, keepdims=True)

    def count(pred):
        def body(c, acc):
            c0 = pl.multiple_of(c * sc, sc)
            s = s_ref[:, pl.ds(c0, sc)]
            col = (lane + c0).astype(F32)
            return acc + _fold_lanes(jnp.where(pred(s, col), 1.0, 0.0), jnp.add)
        acc = lax.fori_loop(0, n_sc, body, jnp.zeros((qn, LANES), F32))
        return jnp.sum(acc, axis=1, keepdims=True)

    n_adm = (lax.broadcasted_iota(jnp.int32, (qn, 1), 0) + qb * qn + 1).astype(F32)
    thr, jmax = _select_topk(count, smin, smax, n_adm, n_sel, seq)

    def mask_chunk(c, _):
        c0 = pl.multiple_of(c * sc, sc)
        s = s_ref[:, pl.ds(c0, sc)]
        col = (lane + c0).astype(F32)
        sel = jnp.logical_or(s > thr, jnp.logical_and(s == thr, col <= jmax))
        sel = jnp.logical_and(sel, s > NEG_INF)
        s_ref[:, pl.ds(c0, sc)] = jnp.where(sel, 0.0, NEG_INF)
        return 0

    lax.fori_loop(0, n_sc, mask_chunk, 0)

    n_far = jnp.maximum(qb - 1, 0)
    wc = WIDE_CHUNK
    n_wide = n_far // (wc // qn)
    tail0 = pl.multiple_of(n_wide * wc, wc)
    tw = TAIL_BLOCKS * qn
    near0 = pl.multiple_of(tail0 + jnp.where(qb > 0, n_far % (wc // qn), 0) * qn, qn)
    toe0 = pl.multiple_of(jnp.where(qb > 0, 0, qn), qn)
    s_ref[:, pl.ds(pl.multiple_of(n_sc * sc, sc), qn)] = jnp.full((qn, qn), NEG_INF, F32)
    lane_p = lax.broadcasted_iota(jnp.int32, (qn, LANES), 1)

    def pair_body(pr, _):
        p0 = pl.multiple_of(pr * LANES, LANES)
        qp = q_ref[:, pl.ds(p0, LANES)]
        zero = jnp.zeros_like(qp)
        qs = jnp.concatenate([jnp.where(lane_p < HEAD_DIM, qp, zero),
                              jnp.where(lane_p >= HEAD_DIM, qp, zero)], axis=0)

        def logits(c0, width):
            lg = _dot(qs, kt_ref[0, pl.ds(p0, LANES), pl.ds(c0, width)])
            m = s_ref[:, pl.ds(c0, width)]
            return lg + jnp.concatenate([m, m], axis=0)

        def p1(c0, width, mrun):
            lg = logits(c0, width)
            lg_ref[:, pl.ds(c0, width)] = lg
            return jnp.maximum(mrun, _fold_lanes(lg, jnp.maximum))

        def p2(c0, width, carry):
            lrun, acc = carry
            p = jnp.exp2(lg_ref[:, pl.ds(c0, width)] - m)
            lrun = lrun + _fold_lanes(p, jnp.add)
            acc = acc + _dot(p.astype(BF16), v_ref[0, pl.ds(c0, width), pl.ds(p0, LANES)])
            return lrun, acc

        def over_wide(step, carry):
            done = 0
            for mult in (4, 2, 1):
                width = mult * wc
                trips = n_wide // 4 if mult == 4 else (n_wide // mult) % 2

                def body(i, c, width=width, base=done):
                    return step(pl.multiple_of(base + i * width, wc), width, c)

                carry = lax.fori_loop(0, trips, body, carry)
                done = done + trips * width
            return carry

        mrun = over_wide(p1, jnp.full((2 * qn, LANES), NEG_INF, F32))
        lg_ref[:, pl.ds(tail0, tw)] = logits(tail0, tw)
        lg_ref[:, pl.ds(near0, 2 * qn)] = lg_ref[:, pl.ds(near0, 2 * qn)] + toe_ref[pr, :, pl.ds(toe0, 2 * qn)]
        mrun = jnp.maximum(mrun, _fold_lanes(lg_ref[:, pl.ds(tail0, tw)], jnp.maximum))
        m = jnp.max(mrun, axis=1, keepdims=True)

        carry = over_wide(p2, (jnp.zeros((2 * qn, LANES), F32), jnp.zeros((2 * qn, LANES), F32)))
        lrun, acc = p2(tail0, tw, carry)
        o = acc / jnp.sum(lrun, axis=1, keepdims=True)
        oacc_ref[:, pl.ds(p0, LANES)] = jnp.where(lane_p < HEAD_DIM, o[:qn], o[qn:])
        return 0

    lax.fori_loop(0, N_PAIRS, pair_body, 0)

    o_ref[...] = x_ref[...] + _dot(oacc_ref[...].astype(BF16), wo_ref[...])


def _attn_prompt(x, q, kt, vb, qi4, kw, kit, toe, wo, batch, seq):
    d = x.shape[1]
    a = ATTN_WIDTH
    nqb = seq // Q_BLOCK
    n_sel = min(TOPK_MAX, seq // 4)
    blk = lambda b, i: (b * nqb + i, 0)
    per_b3 = lambda b, i: (b, 0, 0)
    seqp = kt.shape[2]
    assert seqp >= seq + (TAIL_BLOCKS - WIDE_CHUNK // Q_BLOCK) * Q_BLOCK
    resident = 2 * seqp * a * 2 + Q_BLOCK * seqp * 4 + 2 * Q_BLOCK * seqp * 4
    return pl.pallas_call(
        functools.partial(_attn_prompt_body, seq=seq, n_sel=n_sel),
        grid=(batch, nqb),
        in_specs=[
            pl.BlockSpec((1, 1, IDX_HEADS * Q_BLOCK, IDX_DIM), lambda b, i: (b, i, 0, 0)),
            pl.BlockSpec((Q_BLOCK, LANES), blk),
            pl.BlockSpec((1, IDX_DIM, seq), per_b3),
            pl.BlockSpec((Q_BLOCK, a), blk),
            pl.BlockSpec((1, a, seqp), per_b3, pipeline_mode=pl.Buffered(1)),
            pl.BlockSpec((1, seqp, a), per_b3, pipeline_mode=pl.Buffered(1)),
            pl.BlockSpec((N_PAIRS, 2 * Q_BLOCK, 3 * Q_BLOCK), lambda b, i: (0, 0, 0), pipeline_mode=pl.Buffered(1)),
            pl.BlockSpec((Q_BLOCK, d), blk),
            pl.BlockSpec((a, d), lambda b, i: (0, 0), pipeline_mode=pl.Buffered(1)),
        ],
        out_specs=pl.BlockSpec((Q_BLOCK, d), blk),
        out_shape=jax.ShapeDtypeStruct(x.shape, F32),
        scratch_shapes=[
            pltpu.VMEM((Q_BLOCK, seqp), F32),
            pltpu.VMEM((2 * Q_BLOCK, seqp), F32),
            pltpu.VMEM((Q_BLOCK, a), F32),
        ],
        compiler_params=pltpu.CompilerParams(
            dimension_semantics=("arbitrary", "arbitrary"),
            vmem_limit_bytes=_vmem_limit(resident + 14 * 1024 * 1024)),
        name="attn_prompt",
    )(qi4, kw, kit, q, kt, vb, toe, x, wo)


def _sample_scores_body(pt_ref, qi_ref, wb_ref, *refs, n_grp, dec_seq):
    page_refs = refs[:n_grp]
    kinew_ref, o_ref, onew_ref = refs[n_grp:]
    qi = qi_ref[0]
    wb = wb_ref[0]

    def score(keys_t, n_tiles):
        s = _dot(qi, keys_t.astype(BF16))
        t = jnp.maximum(s, 0.0) * jnp.concatenate([wb] * n_tiles, axis=1)
        tot = t[:dec_seq]
        for h in range(1, IDX_HEADS):
            tot = tot + t[h * dec_seq:(h + 1) * dec_seq]
        return tot

    o_ref[0] = score(jnp.concatenate([r[0] for r in page_refs], axis=1), n_grp)

    @pl.when(pl.program_id(1) == 0)
    def _():
        tot = score(kinew_ref[0], 1)
        j = lax.broadcasted_iota(jnp.int32, tot.shape, 1)
        tok = lax.broadcasted_iota(jnp.int32, tot.shape, 0)
        onew_ref[0] = jnp.where(j <= tok, tot, NEG_INF)


def _sample_scores(page_table, qi_s, wb, cache_kidx, ki_new_pad, dec_seq):
    n, n_pages = page_table.shape
    n_grp = SCORE_PAGES if n_pages % SCORE_PAGES == 0 else 1
    rows = IDX_HEADS * dec_seq
    per_seq = lambda i, g, pt: (i, 0, 0)

    def page(j):
        return lambda i, g, pt: (pt[i, g * n_grp + j], 0, 0)

    grid_spec = pltpu.PrefetchScalarGridSpec(
        num_scalar_prefetch=1,
        grid=(n, n_pages // n_grp),
        in_specs=[pl.BlockSpec((1, rows, IDX_DIM), per_seq), pl.BlockSpec((1, rows, LANES), per_seq)]
        + [pl.BlockSpec((1, IDX_DIM, PAGE_SIZE), page(j)) for j in range(n_grp)]
        + [pl.BlockSpec((1, IDX_DIM, PAGE_SIZE), per_seq)],
        out_specs=[pl.BlockSpec((1, dec_seq, n_grp * PAGE_SIZE), lambda i, g, pt: (i, 0, g)),
                   pl.BlockSpec((1, dec_seq, PAGE_SIZE), per_seq)],
    )
    return pl.pallas_call(
        functools.partial(_sample_scores_body, n_grp=n_grp, dec_seq=dec_seq),
        grid_spec=grid_spec,
        out_shape=[jax.ShapeDtypeStruct((n, dec_seq, n_pages * PAGE_SIZE), F32),
                   jax.ShapeDtypeStruct((n, dec_seq, PAGE_SIZE), F32)],
        compiler_params=pltpu.CompilerParams(dimension_semantics=("arbitrary", "arbitrary")),
        name="sample_scores",
    )(page_table, qi_s, wb, *([cache_kidx] * n_grp), ki_new_pad)


def _attn_sample_body(pt_ref, scp_ref, scn_ref, q_ref, *refs, n_pages, n_grp, dec_seq, n_sel):
    k_refs = refs[:n_grp]
    v_refs = refs[n_grp:2 * n_grp]
    (knew_ref, vnew_ref, bias_ref, x_ref, wo_ref,
     o_ref, mask_ref, q16_ref, m_ref, l_ref, acc_ref) = refs[2 * n_grp:]
    g = pl.program_id(1)
    n_groups = n_pages // n_grp
    n_cols = (n_pages + 1) * PAGE_SIZE
    ds = dec_seq

    @pl.when(g == 0)
    def _():
        s = jnp.concatenate([scp_ref[0], scn_ref[0]], axis=1)
        col = lax.broadcasted_iota(jnp.int32, s.shape, 1).astype(F32)
        fin = s > NEG_INF
        smin = jnp.min(jnp.where(fin, s, jnp.inf), axis=1, keepdims=True)
        smax = jnp.max(s, axis=1, keepdims=True)

        def count(pred):
            return jnp.sum(jnp.where(pred(s, col), 1.0, 0.0), axis=1, keepdims=True)

        tok = lax.broadcasted_iota(jnp.int32, (ds, 1), 0)
        n_adm = (tok + n_pages * PAGE_SIZE + 1).astype(F32)
        thr, jmax = _select_topk(count, smin, smax, n_adm, n_sel, n_cols)
        sel = jnp.logical_or(s > thr, jnp.logical_and(s == thr, col <= jmax))
        sel = jnp.logical_and(sel, fin)
        mask_ref[...] = jnp.where(sel, 0.0, NEG_INF)
        q = q_ref[0]
        for h in range(N_HEADS):
            qh = q[:, h * HEAD_DIM:(h + 1) * HEAD_DIM]
            q16_ref[h] = jnp.concatenate([qh, jnp.zeros_like(qh)], axis=0).astype(BF16)
        m_ref[...] = jnp.full(m_ref.shape, NEG_INF, F32)
        l_ref[...] = jnp.zeros(l_ref.shape, F32)
        acc_ref[...] = jnp.zeros(acc_ref.shape, F32)

    def head_t(page_refs, h):
        return jnp.concatenate([r[0, h] for r in page_refs], axis=1).astype(BF16)

    def pad_rows16(x):
        return jnp.concatenate([x, jnp.zeros_like(x)], axis=0).astype(BF16)

    def attend(k_refs, v_refs, mk, bias):
        lg = jnp.concatenate([_dot(q16_ref[h], head_t(k_refs, h))[:ds] for h in range(N_HEADS)], axis=0)
        lg = lg + jnp.concatenate([mk] * N_HEADS, axis=0) + bias
        m_old = m_ref[...]
        m_new = jnp.maximum(m_old, jnp.max(lg, axis=1, keepdims=True))
        m_safe = jnp.where(m_new == NEG_INF, 0.0, m_new)
        alpha = jnp.exp(m_old - m_safe)
        pr = jnp.exp(lg - m_safe)
        l_ref[...] = alpha * l_ref[...] + jnp.sum(pr, axis=1, keepdims=True)
        pv = jnp.concatenate([_dot_nt(pad_rows16(pr[h * ds:(h + 1) * ds]), head_t(v_refs, h))[:ds]
                              for h in range(N_HEADS)], axis=0)
        acc_ref[...] = alpha * acc_ref[...] + pv
        m_ref[...] = m_new

    @pl.when(g < n_groups)
    def _():
        c0 = pl.multiple_of(g * (n_grp * PAGE_SIZE), n_grp * PAGE_SIZE)
        mk = mask_ref[:, pl.ds(c0, n_grp * PAGE_SIZE)]
        near = jnp.where(g == n_groups - 1, bias_ref[1], 0.0)
        if n_grp > 1:
            near = jnp.concatenate([jnp.zeros((near.shape[0], (n_grp - 1) * PAGE_SIZE), F32), near], axis=1)
        attend(k_refs, v_refs, mk, near)

    @pl.when(g == n_groups)
    def _():
        attend([knew_ref], [vnew_ref], mask_ref[:, pl.ds(n_pages * PAGE_SIZE, PAGE_SIZE)], bias_ref[2])
        o = acc_ref[...] / l_ref[...]
        out = jnp.zeros((2 * ds, wo_ref.shape[1]), F32)
        for h in range(N_HEADS):
            out = out + _dot(pad_rows16(o[h * ds:(h + 1) * ds]), wo_ref[h * HEAD_DIM:(h + 1) * HEAD_DIM, :])
        o_ref[0] = x_ref[0] + out[:ds]


def _attn_sample(page_table, scores_past, scores_new, q_s, cache_k, cache_v, k_new, v_new, bias3, x_s, wo, dec_seq):
    n, n_pages = page_table.shape
    n_grp = ATTN_PAGES if n_pages % ATTN_PAGES == 0 else 1
    n_groups = n_pages // n_grp
    a = ATTN_WIDTH
    d = x_s.shape[-1]
    rows = N_HEADS * dec_seq
    page_shape = (1, N_HEADS, HEAD_DIM, PAGE_SIZE)
    n_cols = (n_pages + 1) * PAGE_SIZE
    n_sel = min(TOPK_MAX, (n_pages * PAGE_SIZE + dec_seq) // 4)
    per_seq = lambda i, g, pt: (i, 0, 0)
    per_seq4 = lambda i, g, pt: (i, 0, 0, 0)

    def page(j):
        return lambda i, g, pt: (pt[i, jnp.minimum(g, n_groups - 1) * n_grp + j], 0, 0, 0)

    page_specs = [pl.BlockSpec(page_shape, page(j)) for j in range(n_grp)]
    grid_spec = pltpu.PrefetchScalarGridSpec(
        num_scalar_prefetch=1,
        grid=(n, n_groups + 1),
        in_specs=[
            pl.BlockSpec((1, dec_seq, n_pages * PAGE_SIZE), per_seq),
            pl.BlockSpec((1, dec_seq, PAGE_SIZE), per_seq),
            pl.BlockSpec((1, dec_seq, a), per_seq),
        ] + page_specs + page_specs + [
            pl.BlockSpec(page_shape, per_seq4),
            pl.BlockSpec(page_shape, per_seq4),
            pl.BlockSpec((3, rows, PAGE_SIZE), lambda i, g, pt: (0, 0, 0)),
            pl.BlockSpec((1, dec_seq, d), per_seq),
            pl.BlockSpec((a, d), lambda i, g, pt: (0, 0)),
        ],
        out_specs=pl.BlockSpec((1, dec_seq, d), per_seq),
        scratch_shapes=[
            pltpu.VMEM((dec_seq, n_cols), F32),
            pltpu.VMEM((N_HEADS, 2 * dec_seq, HEAD_DIM), BF16),
            pltpu.VMEM((rows, 1), F32),
            pltpu.VMEM((rows, 1), F32),
            pltpu.VMEM((rows, HEAD_DIM), F32),
        ],
    )
    return pl.pallas_call(
        functools.partial(_attn_sample_body, n_pages=n_pages, n_grp=n_grp, dec_seq=dec_seq, n_sel=n_sel),
        grid_spec=grid_spec,
        out_shape=jax.ShapeDtypeStruct(x_s.shape, F32),
        compiler_params=pltpu.CompilerParams(
            dimension_semantics=("arbitrary", "arbitrary"),
            vmem_limit_bytes=_vmem_limit(44 * 1024 * 1024)),
        name="attn_sample",
    )(page_table, scores_past, scores_new, q_s, *([cache_k] * n_grp), *([cache_v] * n_grp),
      k_new, v_new, bias3, x_s, wo)


def _gelu_tanh(x):
    return 0.5 * x * (1.0 + jnp.tanh(math.sqrt(2.0 / math.pi) * (x + 0.044715 * (x * x * x))))


def _lru_body(*refs, sample, tm, width):
    if sample:
        (x_ref, g_ref, win_ref, cw_ref, cb_ref, wax_ref, bax_ref, lam_ref, wout_ref,
         r1_ref, r2_ref, r3_ref, h0_ref, o_ref, xr_ref, hs_ref, a_ref, u_ref) = refs
    else:
        (x_ref, g_ref, win_ref, cw_ref, cb_ref, wax_ref, bax_ref, lam_ref, wout_ref,
         o_ref, xr_ref, hs_ref, a_ref, u_ref, prev_ref, hc_ref) = refs

        @pl.when(pl.program_id(1) == 0)
        def _():
            prev_ref[...] = jnp.zeros(prev_ref.shape, F32)
            hc_ref[...] = jnp.zeros(hc_ref.shape, F32)

    w = width
    x = x_ref[...]
    h = _rms(x, g_ref[...]).astype(BF16)
    z = _dot(h, win_ref[...])
    gate = _gelu_tanh(z[:, :w])
    xr = z[:, w:]
    row = lax.broadcasted_iota(jnp.int32, (tm, w), 0)
    pos8 = row & (SUBLANES - 1)

    shifted = []
    for k in range(1, CONV_WIDTH):
        rolled = pltpu.roll(xr, k, 0)
        if sample:
            first = (r1_ref, r2_ref, r3_ref)[k - 1][...]
            shifted.append(jnp.where(pos8 < k, first, rolled))
        else:
            first = jnp.concatenate([pltpu.roll(prev_ref[...], k, 0)] * (tm // SUBLANES), axis=0)
            shifted.append(jnp.where(row < k, first, rolled))
    xc = cb_ref[...]
    for j in range(CONV_WIDTH - 1):
        xc = xc + shifted[CONV_WIDTH - 2 - j] * cw_ref[j:j + 1, :]
    xc = xc + xr * cw_ref[CONV_WIDTH - 1:CONV_WIDTH, :]

    lam = lam_ref[...]
    nl = -lam
    softplus = jnp.maximum(nl, 0.0) + jnp.log1p(jnp.exp(-jnp.abs(nl)))
    xcb = xc.astype(BF16)
    bw = w // LRU_BLOCKS
    for gi in range(LRU_BLOCKS):
        sl = slice(gi * bw, (gi + 1) * bw)
        ga = _dot(xcb[:, sl], wax_ref[gi]) + bax_ref[gi]
        r = _sigmoid(ga[:, :bw])
        ig = _sigmoid(ga[:, bw:])
        log_a = -LRU_C * r * softplus[:, sl]
        a_ref[:, sl] = jnp.exp(log_a)
        th = jnp.tanh(log_a)
        u_ref[:, sl] = jnp.sqrt(-2.0 * th / (1.0 - th)) * (ig * xc[:, sl])

    a_c = a_ref[...]
    u_c = u_ref[...]
    for s in (1, 2, 4):
        ok = pos8 >= s
        a_sh = jnp.where(ok, pltpu.roll(a_c, s, 0), 1.0)
        u_sh = jnp.where(ok, pltpu.roll(u_c, s, 0), 0.0)
        u_c = u_c + a_c * u_sh
        a_c = a_c * a_sh
    if sample:
        hs = u_c + a_c * h0_ref[...]
    else:
        carry = hc_ref[...]
        parts = []
        for gi in range(tm // SUBLANES):
            sl = slice(gi * SUBLANES, (gi + 1) * SUBLANES)
            hg = u_c[sl] + a_c[sl] * carry
            parts.append(hg)
            carry = jnp.broadcast_to(hg[SUBLANES - 1:SUBLANES], (SUBLANES, w))
        hs = jnp.concatenate(parts, axis=0)
        hc_ref[...] = carry
        prev_ref[...] = xr[tm - SUBLANES:]

    y = (hs * gate).astype(BF16)
    o_ref[...] = x + _dot(y, wout_ref[...])
    if sample:
        xr_ref[...] = xr
        hs_ref[...] = hs
    else:
        xr_ref[0] = xr[tm - SUBLANES:]
        hs_ref[0] = hs[tm - SUBLANES:]


def _lru(x, g, win, cw, cb, wax, bax, lam, wout, batch=None, seq=None, sample_state=None):
    t, d = x.shape
    w = wout.shape[0]
    bw = w // LRU_BLOCKS
    sample = sample_state is not None
    const2 = lambda *_: (0, 0)
    const3 = lambda *_: (0, 0, 0)
    single = pl.Buffered(1)
    if sample:
        tm = t
        grid = (1,)
        row = lambda i: (0, 0)
        dims = ("arbitrary",)
    else:
        tm = 256
        nt = seq // tm
        grid = (batch, nt)
        row = lambda b, i: (b * nt + i, 0)
        dims = ("arbitrary", "arbitrary")
    in_specs = [
        pl.BlockSpec((tm, d), row),
        pl.BlockSpec((1, d), const2),
        pl.BlockSpec((d, 2 * w), const2, pipeline_mode=single),
        pl.BlockSpec((CONV_WIDTH, w), const2),
        pl.BlockSpec((1, w), const2),
        pl.BlockSpec((LRU_BLOCKS, bw, 2 * bw), const3),
        pl.BlockSpec((LRU_BLOCKS, 1, 2 * bw), const3),
        pl.BlockSpec((1, w), const2),
        pl.BlockSpec((w, d), const2, pipeline_mode=single),
    ]
    args = [x, g.reshape(1, d), win, cw, cb.reshape(1, w), wax, bax, lam.reshape(1, w), wout]
    scratch = [pltpu.VMEM((tm, w), F32), pltpu.VMEM((tm, w), F32)]
    if sample:
        in_specs += [pl.BlockSpec((tm, w), row)] * 4
        args += list(sample_state)
        out_specs = [pl.BlockSpec((tm, d), row), pl.BlockSpec((tm, w), row), pl.BlockSpec((tm, w), row)]
        out_shape = [jax.ShapeDtypeStruct((t, d), F32), jax.ShapeDtypeStruct((t, w), F32),
                     jax.ShapeDtypeStruct((t, w), F32)]
    else:
        tail = lambda b, i: (b, 0, 0)
        out_specs = [pl.BlockSpec((tm, d), row), pl.BlockSpec((1, SUBLANES, w), tail),
                     pl.BlockSpec((1, SUBLANES, w), tail)]
        out_shape = [jax.ShapeDtypeStruct((t, d), F32), jax.ShapeDtypeStruct((batch, SUBLANES, w), F32),
                     jax.ShapeDtypeStruct((batch, SUBLANES, w), F32)]
        scratch += [pltpu.VMEM((SUBLANES, w), F32), pltpu.VMEM((SUBLANES, w), F32)]
    return pl.pallas_call(
        functools.partial(_lru_body, sample=sample, tm=tm, width=w),
        grid=grid,
        in_specs=in_specs,
        out_specs=out_specs,
        out_shape=out_shape,
        scratch_shapes=scratch,
        compiler_params=pltpu.CompilerParams(
            dimension_semantics=dims,
            vmem_limit_bytes=_vmem_limit(40 * 1024 * 1024)),
        name="lru_sample" if sample else "lru_prompt",
    )(*args)


def kernel(x_prompt, x_sample, cache_k, cache_v, cache_kidx, page_table, state_conv, state_h, rel_bias, attn_w_in, attn_w_out, lru_w_in, lru_conv_w, lru_conv_b, lru_w_a, lru_b_a, lru_w_x, lru_b_x, lru_lambda, lru_w_out, norm_g, ffn_w_gu, ffn_w_down, final_norm_g):
    batch, seq, d = x_prompt.shape
    n_dec, dec_seq, _ = x_sample.shape
    n_pages = page_table.shape[1]
    n_pool = cache_k.shape[0]
    a = ATTN_WIDTH
    w = lru_w_out.shape[0]
    assert seq % SCORE_CHUNK == 0 and dec_seq == SUBLANES and page_table.shape[1] >= 2

    xp = x_prompt.reshape(batch * seq, d)
    xs = x_sample.reshape(n_dec * dec_seq, d)

    wgu = ffn_w_gu.astype(BF16)
    wdn = ffn_w_down.astype(BF16)
    qi_end = 3 * a + IDX_HEADS * IDX_DIM
    wqkv = attn_w_in[:, :3 * a].astype(BF16)
    wqi = attn_w_in[:, 3 * a:qi_end].astype(BF16)
    wkw = jnp.pad(attn_w_in[:, qi_end:], ((0, 0), (0, LANES - (IDX_DIM + IDX_HEADS)))).astype(BF16)
    wo = attn_w_out.astype(BF16)
    lwin = lru_w_in.astype(BF16)
    lwout = lru_w_out.astype(BF16)
    wax = jnp.concatenate([lru_w_a, lru_w_x], axis=-1).astype(BF16)
    bax = jnp.concatenate([lru_b_a, lru_b_x], axis=-1)[:, None, :]

    xp = _ffn(xp, norm_g[0, 0], wgu[0, 0], wdn[0, 0])
    xs = _ffn(xs, norm_g[0, 0], wgu[0, 0], wdn[0, 0])

    q_p, kt_p, vt_p, ktb_p, vb_p, qi_p, kw_p, kwt_p = _attn_proj(
        xp, norm_g[0, 1], wqkv, wqi, wkw, HEAD_DIM ** -0.5 * LOG2E, batch=batch)
    q_s, k_s, v_s, _, _, qi_s, kw_s = _attn_proj(xs, norm_g[0, 1], wqkv, wqi, wkw, HEAD_DIM ** -0.5)

    nqb = seq // Q_BLOCK
    qi4 = qi_p.reshape(batch, nqb, Q_BLOCK, IDX_HEADS, IDX_DIM).transpose(0, 1, 3, 2, 4)
    qi4 = qi4.reshape(batch, nqb, IDX_HEADS * Q_BLOCK, IDX_DIM)
    kit = kwt_p[:, :IDX_DIM, :].astype(BF16)
    kt = jnp.pad(ktb_p, ((0, 0), (0, 0), (0, Q_BLOCK)))
    vb3 = jnp.pad(vb_p.reshape(batch, seq, a), ((0, 0), (0, Q_BLOCK), (0, 0)))
    i_q = np.arange(Q_BLOCK)[:, None]
    c_k = np.arange(3 * Q_BLOCK)[None, :]
    toe = _bias_rows(rel_bias, Q_BLOCK + i_q - c_k)
    toe = toe.reshape(N_PAIRS, 2 * Q_BLOCK, 3 * Q_BLOCK) * LOG2E
    xp = _attn_prompt(xp, q_p, kt, vb3, qi4, kw_p, kit, toe, wo, batch, seq)

    qi_s3 = qi_s.reshape(n_dec, dec_seq, IDX_HEADS, IDX_DIM).transpose(0, 2, 1, 3)
    qi_s3 = qi_s3.reshape(n_dec, IDX_HEADS * dec_seq, IDX_DIM)
    wi_s = kw_s[:, IDX_DIM:IDX_DIM + IDX_HEADS].reshape(n_dec, dec_seq, IDX_HEADS) * (IDX_HEADS ** -0.5)
    wb = jnp.broadcast_to(wi_s.transpose(0, 2, 1).reshape(n_dec, IDX_HEADS * dec_seq, 1),
                          (n_dec, IDX_HEADS * dec_seq, LANES))
    new_keys = PAGE_SIZE - dec_seq
    ki_new = jnp.pad(kw_s[:, :IDX_DIM].reshape(n_dec, dec_seq, IDX_DIM).transpose(0, 2, 1),
                     ((0, 0), (0, 0), (0, new_keys)))
    scores_past, scores_new = _sample_scores(page_table, qi_s3, wb, cache_kidx.transpose(0, 2, 1), ki_new, dec_seq)
    new_pad = ((0, 0), (0, 0), (0, 0), (0, new_keys))
    k_new = jnp.pad(k_s.reshape(n_dec, dec_seq, N_HEADS, HEAD_DIM).transpose(0, 2, 3, 1), new_pad)
    v_new = jnp.pad(v_s.reshape(n_dec, dec_seq, N_HEADS, HEAD_DIM).transpose(0, 2, 3, 1), new_pad)
    tok = np.arange(dec_seq)[:, None]
    jj = np.arange(PAGE_SIZE)[None, :]
    b_last = _bias_rows(rel_bias, PAGE_SIZE + tok - jj)
    b_new = _bias_rows(rel_bias, tok - jj)
    bias3 = jnp.stack([jnp.zeros_like(b_last), b_last, b_new]).reshape(3, N_HEADS * dec_seq, PAGE_SIZE)
    xs3 = _attn_sample(page_table, scores_past, scores_new, q_s.astype(F32).reshape(n_dec, dec_seq, a),
                       cache_k.transpose(0, 2, 3, 1), cache_v.transpose(0, 2, 3, 1), k_new, v_new, bias3, xs.reshape(n_dec, dec_seq, d), wo, dec_seq)
    xs = xs3.reshape(n_dec * dec_seq, d)

    xp = _ffn(xp, norm_g[0, 2], wgu[0, 1], wdn[0, 1])
    xs = _ffn(xs, norm_g[0, 2], wgu[0, 1], wdn[0, 1])

    xp = _ffn(xp, norm_g[1, 0], wgu[1, 0], wdn[1, 0])
    xs = _ffn(xs, norm_g[1, 0], wgu[1, 0], wdn[1, 0])

    lru_args = (lwin, lru_conv_w, lru_conv_b, wax, bax, lru_lambda, lwout)
    xp, conv_p8, h_p8 = _lru(xp, norm_g[1, 1], *lru_args, batch=batch, seq=seq)
    firsts = []
    for k in range(1, CONV_WIDTH):
        r = jnp.pad(state_conv[:, CONV_WIDTH - 1 - k:, :], ((0, 0), (0, dec_seq - k), (0, 0)))
        firsts.append(r.reshape(n_dec * dec_seq, w))
    h0 = jnp.broadcast_to(state_h[:, None, :], (n_dec, dec_seq, w)).reshape(n_dec * dec_seq, w)
    xs, xr_s, hs_s = _lru(xs, norm_g[1, 1], *lru_args, sample_state=firsts + [h0])

    xp = _ffn(xp, norm_g[1, 2], wgu[1, 1], wdn[1, 1], gf=final_norm_g)
    xs = _ffn(xs, norm_g[1, 2], wgu[1, 1], wdn[1, 1], gf=final_norm_g)

    nh, hd = N_HEADS, HEAD_DIM
    keep = CONV_WIDTH - 1
    return (
        xp.reshape(batch, seq, d),
        xs.reshape(n_dec, dec_seq, d),
        kt_p.reshape(batch, nh, hd, seq).transpose(0, 3, 1, 2),
        vt_p.reshape(batch, nh, hd, seq).transpose(0, 3, 1, 2),
        kwt_p[:, :IDX_DIM, :].transpose(0, 2, 1),
        k_s.reshape(n_dec, dec_seq, nh, hd),
        v_s.reshape(n_dec, dec_seq, nh, hd),
        kw_s[:, :IDX_DIM].reshape(n_dec, dec_seq, IDX_DIM),
        conv_p8[:, SUBLANES - keep:, :],
        h_p8[:, SUBLANES - 1, :],
        xr_s.reshape(n_dec, dec_seq, w)[:, dec_seq - keep:, :],
        hs_s.reshape(n_dec, dec_seq, w)[:, dec_seq - 1, :],
    )
```

```python
import functools
import math

import numpy as np
import jax
import jax.numpy as jnp
from jax import lax
from jax.experimental import pallas as pl
from jax.experimental.pallas import tpu as pltpu

F32 = jnp.float32
BF16 = jnp.bfloat16
NEG_INF = float("-inf")

N_HEADS = 16
HEAD_DIM = 64
ATTN_WIDTH = N_HEADS * HEAD_DIM
IDX_HEADS = 8
IDX_DIM = 64
TOPK_MAX = 256
Q_BLOCK = 128
PAGE_SIZE = 128
N_BUCKETS = 32
MAX_DISTANCE = 128
LRU_BLOCKS = 8
CONV_WIDTH = 4
LRU_C = 8.0
RMS_EPS = 1e-6
LOG2E = math.log2(math.e)

V7X_VMEM_BYTES = 64 * 1024 * 1024
LANES = 128
SUBLANES = 8

N_PAIRS = N_HEADS // 2
SCORE_CHUNK = 512
WIDE_CHUNK = 512
SCORE_PAGES = 8
ATTN_PAGES = 8
TAIL_BLOCKS = 5
MAX_BISECT_ITERS = 320


def _vmem_limit(nbytes):
    return int(min(nbytes, V7X_VMEM_BYTES - 4 * 1024 * 1024))


def _rms(x, g):
    ms = jnp.mean(x * x, axis=-1, keepdims=True)
    return x * lax.rsqrt(ms + RMS_EPS) * g


def _sigmoid(x):
    return 1.0 / (1.0 + jnp.exp(-x))


def _dot(a, b):
    return jnp.dot(a, b, preferred_element_type=F32)


def _dot_nt(a, b):
    return lax.dot_general(a, b, (((1,), (1,)), ((), ())), preferred_element_type=F32)


def _fold_lanes(x, op):
    out = x[:, :LANES]
    for j in range(1, x.shape[1] // LANES):
        out = op(out, x[:, j * LANES:(j + 1) * LANES])
    return out


def _ffn_body(*refs, d_ff, fc, final):
    if final:
        x_ref, g_ref, wgu_ref, wd_ref, gf_ref, o_ref = refs
    else:
        x_ref, g_ref, wgu_ref, wd_ref, o_ref = refs
    x = x_ref[...]
    h = _rms(x, g_ref[...]).astype(BF16)
    acc = jnp.zeros(x.shape, F32)
    for c in range(d_ff // fc):
        gg = _dot(h, wgu_ref[:, c * fc:(c + 1) * fc])
        uu = _dot(h, wgu_ref[:, d_ff + c * fc:d_ff + (c + 1) * fc])
        a = (gg * _sigmoid(gg) * uu).astype(BF16)
        acc = acc + _dot(a, wd_ref[c * fc:(c + 1) * fc, :])
    y = x + 0.5 * acc
    if final:
        y = _rms(y, gf_ref[...])
    o_ref[...] = y


def _ffn(x, g, wgu, wd, gf=None):
    t, d = x.shape
    d_ff = wd.shape[0]
    tm = 512 if t % 512 == 0 else 256
    fc = 256
    final = gf is not None
    const = lambda i: (0, 0)
    in_specs = [
        pl.BlockSpec((tm, d), lambda i: (i, 0)),
        pl.BlockSpec((1, d), const),
        pl.BlockSpec((d, 2 * d_ff), const, pipeline_mode=pl.Buffered(1)),
        pl.BlockSpec((d_ff, d), const, pipeline_mode=pl.Buffered(1)),
    ]
    args = [x, g.reshape(1, d), wgu, wd]
    if final:
        in_specs.append(pl.BlockSpec((1, d), const))
        args.append(gf.reshape(1, d))
    return pl.pallas_call(
        functools.partial(_ffn_body, d_ff=d_ff, fc=fc, final=final),
        grid=(t // tm,),
        in_specs=in_specs,
        out_specs=pl.BlockSpec((tm, d), lambda i: (i, 0)),
        out_shape=jax.ShapeDtypeStruct((t, d), F32),
        compiler_params=pltpu.CompilerParams(
            dimension_semantics=("arbitrary",),
            vmem_limit_bytes=_vmem_limit(48 * 1024 * 1024)),
        name="ffn_final" if final else "ffn",
    )(*args)


def _proj_body(x_ref, g_ref, wqkv_ref, wqi_ref, wkw_ref,
               q_ref, k_ref, v_ref, kb_ref, vb_ref, qi_ref, kw_ref, *maybe_kwt_ref, q_scale, key_minor):
    h = _rms(x_ref[...], g_ref[...]).astype(BF16)
    a = ATTN_WIDTH
    q_ref[...] = (_dot(h, wqkv_ref[:, :a]) * q_scale).astype(BF16)
    k = _dot(h, wqkv_ref[:, a:2 * a])
    v = _dot(h, wqkv_ref[:, 2 * a:])
    kw = _dot(h, wkw_ref[...])
    if key_minor:
        kt = k.T
        k_ref[0] = kt
        kb_ref[0] = kt.astype(BF16)
        v_ref[0] = v.T
        maybe_kwt_ref[0][0] = kw.T
    else:
        k_ref[...] = k
        kb_ref[...] = k.astype(BF16)
        v_ref[...] = v
    vb_ref[...] = v.astype(BF16)
    qi_ref[...] = (_dot(h, wqi_ref[...]) * (IDX_DIM ** -0.5)).astype(BF16)
    kw_ref[...] = kw


def _attn_proj(x, g, wqkv, wqi, wkw, q_scale, batch=None):
    t, d = x.shape
    tm = 256
    a = ATTN_WIDTH
    nqi = IDX_HEADS * IDX_DIM
    const = lambda i: (0, 0)
    row = lambda i: (i, 0)
    key_minor = batch is not None
    if key_minor:
        seq = t // batch
        nt = seq // tm
        chan = lambda i: (i // nt, 0, i % nt)
        kv_spec = pl.BlockSpec((1, a, tm), chan)
        kv_f32 = jax.ShapeDtypeStruct((batch, a, seq), F32)
        kv_bf16 = jax.ShapeDtypeStruct((batch, a, seq), BF16)
    else:
        kv_spec = pl.BlockSpec((tm, a), row)
        kv_f32 = jax.ShapeDtypeStruct((t, a), F32)
        kv_bf16 = jax.ShapeDtypeStruct((t, a), BF16)
    out_specs = [
        pl.BlockSpec((tm, a), row), kv_spec, kv_spec, kv_spec, pl.BlockSpec((tm, a), row),
        pl.BlockSpec((tm, nqi), row), pl.BlockSpec((tm, LANES), row),
    ]
    out_shape = [
        jax.ShapeDtypeStruct((t, a), BF16), kv_f32, kv_f32, kv_bf16, jax.ShapeDtypeStruct((t, a), BF16),
        jax.ShapeDtypeStruct((t, nqi), BF16), jax.ShapeDtypeStruct((t, LANES), F32),
    ]
    if key_minor:
        out_specs.append(pl.BlockSpec((1, LANES, tm), chan))
        out_shape.append(jax.ShapeDtypeStruct((batch, LANES, seq), F32))
    return pl.pallas_call(
        functools.partial(_proj_body, q_scale=q_scale, key_minor=key_minor),
        grid=(t // tm,),
        in_specs=[
            pl.BlockSpec((tm, d), row),
            pl.BlockSpec((1, d), const),
            pl.BlockSpec((d, 3 * a), const, pipeline_mode=pl.Buffered(1)),
            pl.BlockSpec((d, nqi), const, pipeline_mode=pl.Buffered(1)),
            pl.BlockSpec((d, LANES), const, pipeline_mode=pl.Buffered(1)),
        ],
        out_specs=out_specs,
        out_shape=out_shape,
        compiler_params=pltpu.CompilerParams(
            dimension_semantics=("arbitrary",),
            vmem_limit_bytes=_vmem_limit(44 * 1024 * 1024)),
        name="attn_proj",
    )(x, g.reshape(1, d), wqkv, wqi, wkw)


def _select_topk(count, smin, smax, n_adm, n_sel, n_cols):
    ksel = float(n_sel)
    c_hi = count(lambda s, col: s >= smax)
    few = n_adm <= ksel
    top_tied = jnp.logical_and(jnp.logical_not(few), c_hi >= ksel)
    lo0 = jnp.where(top_tied, smax, smin)
    done0 = jnp.where(jnp.logical_or(few, top_tied), 1.0, 0.0)

    def not_all_done(done):
        return (jnp.min(done) < 0.5).astype(jnp.int32)

    def cond(carry):
        return jnp.logical_and(carry[3] > 0, carry[4] < MAX_BISECT_ITERS)

    def body(carry):
        lo, hi, done, _, it = carry
        mid = lo * 0.5 + hi * 0.5
        conv = jnp.logical_or(mid <= lo, mid >= hi)
        c = count(lambda s, col: s >= mid)
        ge = c >= ksel
        upd = jnp.logical_and(done < 0.5, jnp.logical_not(conv))
        lo = jnp.where(jnp.logical_and(upd, ge), mid, lo)
        hi = jnp.where(jnp.logical_and(upd, jnp.logical_not(ge)), mid, hi)
        done = jnp.where(jnp.logical_or(conv, c == ksel), 1.0, done)
        return lo, hi, done, not_all_done(done), it + 1

    thr = lax.while_loop(cond, body, (lo0, smax, done0, not_all_done(done0), jnp.int32(0)))[0]

    c_ge = count(lambda s, col: s >= thr)
    has_tie = c_ge > ksel
    big = float(n_cols)

    def tie_break(_):
        c_gt = count(lambda s, col: s > thr)
        need = ksel - c_gt
        jlo = jnp.full_like(thr, -1.0)
        jhi = jnp.full_like(thr, big)

        def jbody(_, jc):
            jlo, jhi = jc
            jm = jnp.floor((jlo + jhi) * 0.5)
            c = count(lambda s, col: jnp.logical_and(s == thr, col <= jm))
            ok = c >= need
            return jnp.where(ok, jlo, jm), jnp.where(ok, jm, jhi)

        n_it = int(math.ceil(math.log2(n_cols + 2))) + 1
        _, jhi = lax.fori_loop(0, n_it, jbody, (jlo, jhi))
        return jnp.where(has_tie, jhi, big)

    any_tie = jnp.max(jnp.where(has_tie, 1.0, 0.0)) > 0.5
    jmax = lax.cond(any_tie, tie_break, lambda _: jnp.full_like(thr, big), 0)
    return thr, jmax


def _t5_bucket_np(rel):
    n = np.maximum(rel, 0)
    max_exact = N_BUCKETS // 2
    nf = np.maximum(n, max_exact).astype(np.float32)
    large = max_exact + (np.log(nf / np.float32(max_exact)) / np.float32(math.log(MAX_DISTANCE / max_exact))
                         * (N_BUCKETS - max_exact)).astype(np.int32)
    large = np.minimum(large, N_BUCKETS - 1)
    return np.where(n < max_exact, n, large)


_FAR_BUCKET = int(_t5_bucket_np(np.array([1 << 20]))[0])
assert int(_t5_bucket_np(np.array([Q_BLOCK + 1]))[0]) == _FAR_BUCKET


def _bias_rows(rel_bias, rel):
    bucket = _t5_bucket_np(rel).reshape(-1)
    onehot = (jnp.asarray(bucket)[None, :] == jnp.arange(N_BUCKETS, dtype=jnp.int32)[:, None]).astype(F32)
    centred = (rel_bias - rel_bias[_FAR_BUCKET][None, :]).T
    b = jnp.dot(centred, onehot, precision=lax.Precision.HIGHEST)
    return b.reshape((N_HEADS,) + rel.shape)


def _attn_prompt_body(qi_ref, kw_ref, kit_ref, q_ref, kt_ref, v_ref, toe_ref, x_ref, wo_ref,
                      o_ref, s_ref, lg_ref, oacc_ref, *, seq, n_sel):
    qb = pl.program_id(1)
    qn = Q_BLOCK
    sc = SCORE_CHUNK
    n_sc = (qb + sc // qn) // (sc // qn)

    qi = qi_ref[0, 0]
    wi = kw_ref[:, IDX_DIM:IDX_DIM + IDX_HEADS] * (IDX_HEADS ** -0.5)
    wcols = [jnp.broadcast_to(wi[:, h:h + 1], (qn, sc)) for h in range(IDX_HEADS)]
    qpos = lax.broadcasted_iota(jnp.int32, (qn, sc), 0) + qb * qn
    lane = lax.broadcasted_iota(jnp.int32, (qn, sc), 1)

    def score_chunk(c, carry):
        smin, smax = carry
        c0 = pl.multiple_of(c * sc, sc)
        s = _dot(qi, kit_ref[0, :, pl.ds(c0, sc)])
        tot = jnp.maximum(s[:qn], 0.0) * wcols[0]
        for h in range(1, IDX_HEADS):
            tot = tot + jnp.maximum(s[h * qn:(h + 1) * qn], 0.0) * wcols[h]
        adm = (lane + c0) <= qpos
        s_ref[:, pl.ds(c0, sc)] = jnp.where(adm, tot, NEG_INF)
        smin = jnp.minimum(smin, _fold_lanes(jnp.where(adm, tot, jnp.inf), jnp.minimum))
        smax = jnp.maximum(smax, _fold_lanes(jnp.where(adm, tot, NEG_INF), jnp.maximum))
        return smin, smax

    smin, smax = lax.fori_loop(
        0, n_sc, score_chunk,
        (jnp.full((qn, LANES), jnp.inf, F32), jnp.full((qn, LANES), NEG_INF, F32)))
    smin = jnp.min(smin, axis=1, keepdims=True)
    smax = jnp.max(smax, axis=1, keepdims=True)

    def count(pred):
        def body(c, acc):
            c0 = pl.multiple_of(c * sc, sc)
            s = s_ref[:, pl.ds(c0, sc)]
            col = (lane + c0).astype(F32)
            return acc + _fold_lanes(jnp.where(pred(s, col), 1.0, 0.0), jnp.add)
        acc = lax.fori_loop(0, n_sc, body, jnp.zeros((qn, LANES), F32))
        return jnp.sum(acc, axis=1, keepdims=True)

    n_adm = (lax.broadcasted_iota(jnp.int32, (qn, 1), 0) + qb * qn + 1).astype(F32)
    thr, jmax = _select_topk(count, smin, smax, n_adm, n_sel, seq)

    def mask_chunk(c, _):
        c0 = pl.multiple_of(c * sc, sc)
        s = s_ref[:, pl.ds(c0, sc)]
        col = (lane + c0).astype(F32)
        sel = jnp.logical_or(s > thr, jnp.logical_and(s == thr, col <= jmax))
        sel = jnp.logical_and(sel, s > NEG_INF)
        s_ref[:, pl.ds(c0, sc)] = jnp.where(sel, 0.0, NEG_INF)
        return 0

    lax.fori_loop(0, n_sc, mask_chunk, 0)

    n_far = jnp.maximum(qb - 1, 0)
    wc = WIDE_CHUNK
    n_wide = n_far // (wc // qn)
    tail0 = pl.multiple_of(n_wide * wc, wc)
    tw = TAIL_BLOCKS * qn
    near0 = pl.multiple_of(tail0 + jnp.where(qb > 0, n_far % (wc // qn), 0) * qn, qn)
    toe0 = pl.multiple_of(jnp.where(qb > 0, 0, qn), qn)
    s_ref[:, pl.ds(pl.multiple_of(n_sc * sc, sc), qn)] = jnp.full((qn, qn), NEG_INF, F32)
    lane_p = lax.broadcasted_iota(jnp.int32, (qn, LANES), 1)

    def pair_body(pr, _):
        p0 = pl.multiple_of(pr * LANES, LANES)
        qp = q_ref[:, pl.ds(p0, LANES)]
        zero = jnp.zeros_like(qp)
        qs = jnp.concatenate([jnp.where(lane_p < HEAD_DIM, qp, zero),
                              jnp.where(lane_p >= HEAD_DIM, qp, zero)], axis=0)

        def logits(c0, width):
            lg = _dot(qs, kt_ref[0, pl.ds(p0, LANES), pl.ds(c0, width)])
            m = s_ref[:, pl.ds(c0, width)]
            return lg + jnp.concatenate([m, m], axis=0)

        def p1(c0, width, mrun):
            lg = logits(c0, width)
            lg_ref[:, pl.ds(c0, width)] = lg
            return jnp.maximum(mrun, _fold_lanes(lg, jnp.maximum))

        def p2(c0, width, carry):
            lrun, acc = carry
            sub = wc if width % wc == 0 else width
            for j in range(0, width, sub):
                cj = pl.multiple_of(c0 + j, qn)
                p = jnp.exp2(lg_ref[:, pl.ds(cj, sub)] - m)
                lrun = lrun + _fold_lanes(p, jnp.add)
                acc = acc + _dot(p.astype(BF16), v_ref[0, pl.ds(cj, sub), pl.ds(p0, LANES)])
            return lrun, acc

        def over_wide(step, carry):
            done = 0
            for mult in (4, 2, 1):
                width = mult * wc
                trips = n_wide // 4 if mult == 4 else (n_wide // mult) % 2

                def body(i, c, width=width, base=done):
                    return step(pl.multiple_of(base + i * width, wc), width, c)

                carry = lax.fori_loop(0, trips, body, carry)
                done = done + trips * width
            return carry

        mrun = over_wide(p1, jnp.full((2 * qn, LANES), NEG_INF, F32))
        lg_ref[:, pl.ds(tail0, tw)] = logits(tail0, tw)
        lg_ref[:, pl.ds(near0, 2 * qn)] = lg_ref[:, pl.ds(near0, 2 * qn)] + toe_ref[pr, :, pl.ds(toe0, 2 * qn)]
        mrun = jnp.maximum(mrun, _fold_lanes(lg_ref[:, pl.ds(tail0, tw)], jnp.maximum))
        m = jnp.max(mrun, axis=1, keepdims=True)

        carry = over_wide(p2, (jnp.zeros((2 * qn, LANES), F32), jnp.zeros((2 * qn, LANES), F32)))
        lrun, acc = p2(tail0, tw, carry)
        o = acc / jnp.sum(lrun, axis=1, keepdims=True)
        oacc_ref[:, pl.ds(p0, LANES)] = jnp.where(lane_p < HEAD_DIM, o[:qn], o[qn:])
        return 0

    lax.fori_loop(0, N_PAIRS, pair_body, 0)

    o_ref[...] = x_ref[...] + _dot(oacc_ref[...].astype(BF16), wo_ref[...])


def _attn_prompt(x, q, kt, vb, qi4, kw, kit, toe, wo, batch, seq):
    d = x.shape[1]
    a = ATTN_WIDTH
    nqb = seq // Q_BLOCK
    n_sel = min(TOPK_MAX, seq // 4)
    blk = lambda b, i: (b * nqb + i, 0)
    per_b3 = lambda b, i: (b, 0, 0)
    seqp = kt.shape[2]
    assert seqp >= seq + (TAIL_BLOCKS - WIDE_CHUNK // Q_BLOCK) * Q_BLOCK
    resident = 2 * seqp * a * 2 + Q_BLOCK * seqp * 4 + 2 * Q_BLOCK * seqp * 4
    return pl.pallas_call(
        functools.partial(_attn_prompt_body, seq=seq, n_sel=n_sel),
        grid=(batch, nqb),
        in_specs=[
            pl.BlockSpec((1, 1, IDX_HEADS * Q_BLOCK, IDX_DIM), lambda b, i: (b, i, 0, 0)),
            pl.BlockSpec((Q_BLOCK, LANES), blk),
            pl.BlockSpec((1, IDX_DIM, seq), per_b3),
            pl.BlockSpec((Q_BLOCK, a), blk),
            pl.BlockSpec((1, a, seqp), per_b3, pipeline_mode=pl.Buffered(1)),
            pl.BlockSpec((1, seqp, a), per_b3, pipeline_mode=pl.Buffered(1)),
            pl.BlockSpec((N_PAIRS, 2 * Q_BLOCK, 3 * Q_BLOCK), lambda b, i: (0, 0, 0), pipeline_mode=pl.Buffered(1)),
            pl.BlockSpec((Q_BLOCK, d), blk),
            pl.BlockSpec((a, d), lambda b, i: (0, 0), pipeline_mode=pl.Buffered(1)),
        ],
        out_specs=pl.BlockSpec((Q_BLOCK, d), blk),
        out_shape=jax.ShapeDtypeStruct(x.shape, F32),
        scratch_shapes=[
            pltpu.VMEM((Q_BLOCK, seqp), F32),
            pltpu.VMEM((2 * Q_BLOCK, seqp), F32),
            pltpu.VMEM((Q_BLOCK, a), F32),
        ],
        compiler_params=pltpu.CompilerParams(
            dimension_semantics=("arbitrary", "arbitrary"),
            vmem_limit_bytes=_vmem_limit(resident + 14 * 1024 * 1024)),
        name="attn_prompt",
    )(qi4, kw, kit, q, kt, vb, toe, x, wo)


def _sample_scores_body(pt_ref, qi_ref, wb_ref, *refs, n_grp, dec_seq):
    page_refs = refs[:n_grp]
    kinew_ref, o_ref, onew_ref = refs[n_grp:]
    qi = qi_ref[0]
    wb = wb_ref[0]

    def score(keys_t, n_tiles):
        s = _dot(qi, keys_t.astype(BF16))
        t = jnp.maximum(s, 0.0) * jnp.concatenate([wb] * n_tiles, axis=1)
        tot = t[:dec_seq]
        for h in range(1, IDX_HEADS):
            tot = tot + t[h * dec_seq:(h + 1) * dec_seq]
        return tot

    o_ref[0] = score(jnp.concatenate([r[0] for r in page_refs], axis=1), n_grp)

    @pl.when(pl.program_id(1) == 0)
    def _():
        tot = score(kinew_ref[0], 1)
        j = lax.broadcasted_iota(jnp.int32, tot.shape, 1)
        tok = lax.broadcasted_iota(jnp.int32, tot.shape, 0)
        onew_ref[0] = jnp.where(j <= tok, tot, NEG_INF)


def _sample_scores(page_table, qi_s, wb, cache_kidx, ki_new_pad, dec_seq):
    n, n_pages = page_table.shape
    n_grp = SCORE_PAGES if n_pages % SCORE_PAGES == 0 else 1
    rows = IDX_HEADS * dec_seq
    per_seq = lambda i, g, pt: (i, 0, 0)

    def page(j):
        return lambda i, g, pt: (pt[i, g * n_grp + j], 0, 0)

    grid_spec = pltpu.PrefetchScalarGridSpec(
        num_scalar_prefetch=1,
        grid=(n, n_pages // n_grp),
        in_specs=[pl.BlockSpec((1, rows, IDX_DIM), per_seq), pl.BlockSpec((1, rows, LANES), per_seq)]
        + [pl.BlockSpec((1, IDX_DIM, PAGE_SIZE), page(j)) for j in range(n_grp)]
        + [pl.BlockSpec((1, IDX_DIM, PAGE_SIZE), per_seq)],
        out_specs=[pl.BlockSpec((1, dec_seq, n_grp * PAGE_SIZE), lambda i, g, pt: (i, 0, g)),
                   pl.BlockSpec((1, dec_seq, PAGE_SIZE), per_seq)],
    )
    return pl.pallas_call(
        functools.partial(_sample_scores_body, n_grp=n_grp, dec_seq=dec_seq),
        grid_spec=grid_spec,
        out_shape=[jax.ShapeDtypeStruct((n, dec_seq, n_pages * PAGE_SIZE), F32),
                   jax.ShapeDtypeStruct((n, dec_seq, PAGE_SIZE), F32)],
        compiler_params=pltpu.CompilerParams(dimension_semantics=("arbitrary", "arbitrary")),
        name="sample_scores",
    )(page_table, qi_s, wb, *([cache_kidx] * n_grp), ki_new_pad)


def _attn_sample_body(pt_ref, scp_ref, scn_ref, q_ref, *refs, n_pages, n_grp, dec_seq, n_sel):
    k_refs = refs[:n_grp]
    v_refs = refs[n_grp:2 * n_grp]
    (knew_ref, vnew_ref, bias_ref, x_ref, wo_ref,
     o_ref, mask_ref, q16_ref, m_ref, l_ref, acc_ref) = refs[2 * n_grp:]
    g = pl.program_id(1)
    n_groups = n_pages // n_grp
    n_cols = (n_pages + 1) * PAGE_SIZE
    ds = dec_seq

    @pl.when(g == 0)
    def _():
        s = jnp.concatenate([scp_ref[0], scn_ref[0]], axis=1)
        col = lax.broadcasted_iota(jnp.int32, s.shape, 1).astype(F32)
        fin = s > NEG_INF
        smin = jnp.min(jnp.where(fin, s, jnp.inf), axis=1, keepdims=True)
        smax = jnp.max(s, axis=1, keepdims=True)

        def count(pred):
            return jnp.sum(jnp.where(pred(s, col), 1.0, 0.0), axis=1, keepdims=True)

        tok = lax.broadcasted_iota(jnp.int32, (ds, 1), 0)
        n_adm = (tok + n_pages * PAGE_SIZE + 1).astype(F32)
        thr, jmax = _select_topk(count, smin, smax, n_adm, n_sel, n_cols)
        sel = jnp.logical_or(s > thr, jnp.logical_and(s == thr, col <= jmax))
        sel = jnp.logical_and(sel, fin)
        mask_ref[...] = jnp.where(sel, 0.0, NEG_INF)
        q = q_ref[0]
        for h in range(N_HEADS):
            qh = q[:, h * HEAD_DIM:(h + 1) * HEAD_DIM]
            q16_ref[h] = jnp.concatenate([qh, jnp.zeros_like(qh)], axis=0).astype(BF16)
        m_ref[...] = jnp.full(m_ref.shape, NEG_INF, F32)
        l_ref[...] = jnp.zeros(l_ref.shape, F32)
        acc_ref[...] = jnp.zeros(acc_ref.shape, F32)

    def head_t(page_refs, h):
        return jnp.concatenate([r[0, h] for r in page_refs], axis=1).astype(BF16)

    def pad_rows16(x):
        return jnp.concatenate([x, jnp.zeros_like(x)], axis=0).astype(BF16)

    def attend(k_refs, v_refs, mk, bias):
        lg = jnp.concatenate([_dot(q16_ref[h], head_t(k_refs, h))[:ds] for h in range(N_HEADS)], axis=0)
        lg = lg + jnp.concatenate([mk] * N_HEADS, axis=0) + bias
        m_old = m_ref[...]
        m_new = jnp.maximum(m_old, jnp.max(lg, axis=1, keepdims=True))
        m_safe = jnp.where(m_new == NEG_INF, 0.0, m_new)
        alpha = jnp.exp(m_old - m_safe)
        pr = jnp.exp(lg - m_safe)
        l_ref[...] = alpha * l_ref[...] + jnp.sum(pr, axis=1, keepdims=True)
        pv = jnp.concatenate([_dot_nt(pad_rows16(pr[h * ds:(h + 1) * ds]), head_t(v_refs, h))[:ds]
                              for h in range(N_HEADS)], axis=0)
        acc_ref[...] = alpha * acc_ref[...] + pv
        m_ref[...] = m_new

    @pl.when(g < n_groups)
    def _():
        c0 = pl.multiple_of(g * (n_grp * PAGE_SIZE), n_grp * PAGE_SIZE)
        mk = mask_ref[:, pl.ds(c0, n_grp * PAGE_SIZE)]
        near = jnp.where(g == n_groups - 1, bias_ref[1], 0.0)
        if n_grp > 1:
            near = jnp.concatenate([jnp.zeros((near.shape[0], (n_grp - 1) * PAGE_SIZE), F32), near], axis=1)
        attend(k_refs, v_refs, mk, near)

    @pl.when(g == n_groups)
    def _():
        attend([knew_ref], [vnew_ref], mask_ref[:, pl.ds(n_pages * PAGE_SIZE, PAGE_SIZE)], bias_ref[2])
        o = acc_ref[...] / l_ref[...]
        out = jnp.zeros((2 * ds, wo_ref.shape[1]), F32)
        for h in range(N_HEADS):
            out = out + _dot(pad_rows16(o[h * ds:(h + 1) * ds]), wo_ref[h * HEAD_DIM:(h + 1) * HEAD_DIM, :])
        o_ref[0] = x_ref[0] + out[:ds]


def _attn_sample(page_table, scores_past, scores_new, q_s, cache_k, cache_v, k_new, v_new, bias3, x_s, wo, dec_seq):
    n, n_pages = page_table.shape
    n_grp = ATTN_PAGES if n_pages % ATTN_PAGES == 0 else 1
    n_groups = n_pages // n_grp
    a = ATTN_WIDTH
    d = x_s.shape[-1]
    rows = N_HEADS * dec_seq
    page_shape = (1, N_HEADS, HEAD_DIM, PAGE_SIZE)
    n_cols = (n_pages + 1) * PAGE_SIZE
    n_sel = min(TOPK_MAX, (n_pages * PAGE_SIZE + dec_seq) // 4)
    per_seq = lambda i, g, pt: (i, 0, 0)
    per_seq4 = lambda i, g, pt: (i, 0, 0, 0)

    def page(j):
        return lambda i, g, pt: (pt[i, jnp.minimum(g, n_groups - 1) * n_grp + j], 0, 0, 0)

    page_specs = [pl.BlockSpec(page_shape, page(j)) for j in range(n_grp)]
    grid_spec = pltpu.PrefetchScalarGridSpec(
        num_scalar_prefetch=1,
        grid=(n, n_groups + 1),
        in_specs=[
            pl.BlockSpec((1, dec_seq, n_pages * PAGE_SIZE), per_seq),
            pl.BlockSpec((1, dec_seq, PAGE_SIZE), per_seq),
            pl.BlockSpec((1, dec_seq, a), per_seq),
        ] + page_specs + page_specs + [
            pl.BlockSpec(page_shape, per_seq4),
            pl.BlockSpec(page_shape, per_seq4),
            pl.BlockSpec((3, rows, PAGE_SIZE), lambda i, g, pt: (0, 0, 0)),
            pl.BlockSpec((1, dec_seq, d), per_seq),
            pl.BlockSpec((a, d), lambda i, g, pt: (0, 0)),
        ],
        out_specs=pl.BlockSpec((1, dec_seq, d), per_seq),
        scratch_shapes=[
            pltpu.VMEM((dec_seq, n_cols), F32),
            pltpu.VMEM((N_HEADS, 2 * dec_seq, HEAD_DIM), BF16),
            pltpu.VMEM((rows, 1), F32),
            pltpu.VMEM((rows, 1), F32),
            pltpu.VMEM((rows, HEAD_DIM), F32),
        ],
    )
    return pl.pallas_call(
        functools.partial(_attn_sample_body, n_pages=n_pages, n_grp=n_grp, dec_seq=dec_seq, n_sel=n_sel),
        grid_spec=grid_spec,
        out_shape=jax.ShapeDtypeStruct(x_s.shape, F32),
        compiler_params=pltpu.CompilerParams(
            dimension_semantics=("arbitrary", "arbitrary"),
            vmem_limit_bytes=_vmem_limit(44 * 1024 * 1024)),
        name="attn_sample",
    )(page_table, scores_past, scores_new, q_s, *([cache_k] * n_grp), *([cache_v] * n_grp),
      k_new, v_new, bias3, x_s, wo)


def _gelu_tanh(x):
    return 0.5 * x * (1.0 + jnp.tanh(math.sqrt(2.0 / math.pi) * (x + 0.044715 * (x * x * x))))


def _lru_body(*refs, sample, tm, width):
    if sample:
        (x_ref, g_ref, win_ref, cw_ref, cb_ref, wax_ref, bax_ref, lam_ref, wout_ref,
         r1_ref, r2_ref, r3_ref, h0_ref, o_ref, xr_ref, hs_ref, a_ref, u_ref) = refs
    else:
        (x_ref, g_ref, win_ref, cw_ref, cb_ref, wax_ref, bax_ref, lam_ref, wout_ref,
         o_ref, xr_ref, hs_ref, a_ref, u_ref, prev_ref, hc_ref) = refs

        @pl.when(pl.program_id(1) == 0)
        def _():
            prev_ref[...] = jnp.zeros(prev_ref.shape, F32)
            hc_ref[...] = jnp.zeros(hc_ref.shape, F32)

    w = width
    x = x_ref[...]
    h = _rms(x, g_ref[...]).astype(BF16)
    z = _dot(h, win_ref[...])
    gate = _gelu_tanh(z[:, :w])
    xr = z[:, w:]
    row = lax.broadcasted_iota(jnp.int32, (tm, w), 0)
    pos8 = row & (SUBLANES - 1)

    shifted = []
    for k in range(1, CONV_WIDTH):
        rolled = pltpu.roll(xr, k, 0)
        if sample:
            first = (r1_ref, r2_ref, r3_ref)[k - 1][...]
            shifted.append(jnp.where(pos8 < k, first, rolled))
        else:
            first = jnp.concatenate([pltpu.roll(prev_ref[...], k, 0)] * (tm // SUBLANES), axis=0)
            shifted.append(jnp.where(row < k, first, rolled))
    xc = cb_ref[...]
    for j in range(CONV_WIDTH - 1):
        xc = xc + shifted[CONV_WIDTH - 2 - j] * cw_ref[j:j + 1, :]
    xc = xc + xr * cw_ref[CONV_WIDTH - 1:CONV_WIDTH, :]

    lam = lam_ref[...]
    nl = -lam
    softplus = jnp.maximum(nl, 0.0) + jnp.log1p(jnp.exp(-jnp.abs(nl)))
    xcb = xc.astype(BF16)
    bw = w // LRU_BLOCKS
    for gi in range(LRU_BLOCKS):
        sl = slice(gi * bw, (gi + 1) * bw)
        ga = _dot(xcb[:, sl], wax_ref[gi]) + bax_ref[gi]
        r = _sigmoid(ga[:, :bw])
        ig = _sigmoid(ga[:, bw:])
        log_a = -LRU_C * r * softplus[:, sl]
        a_ref[:, sl] = jnp.exp(log_a)
        th = jnp.tanh(log_a)
        u_ref[:, sl] = jnp.sqrt(-2.0 * th / (1.0 - th)) * (ig * xc[:, sl])

    a_c = a_ref[...]
    u_c = u_ref[...]
    for s in (1, 2, 4):
        ok = pos8 >= s
        a_sh = jnp.where(ok, pltpu.roll(a_c, s, 0), 1.0)
        u_sh = jnp.where(ok, pltpu.roll(u_c, s, 0), 0.0)
        u_c = u_c + a_c * u_sh
        a_c = a_c * a_sh
    if sample:
        hs = u_c + a_c * h0_ref[...]
    else:
        carry = hc_ref[...]
        parts = []
        for gi in range(tm // SUBLANES):
            sl = slice(gi * SUBLANES, (gi + 1) * SUBLANES)
            hg = u_c[sl] + a_c[sl] * carry
            parts.append(hg)
            carry = jnp.broadcast_to(hg[SUBLANES - 1:SUBLANES], (SUBLANES, w))
        hs = jnp.concatenate(parts, axis=0)
        hc_ref[...] = carry
        prev_ref[...] = xr[tm - SUBLANES:]

    y = (hs * gate).astype(BF16)
    o_ref[...] = x + _dot(y, wout_ref[...])
    if sample:
        xr_ref[...] = xr
        hs_ref[...] = hs
    else:
        xr_ref[0] = xr[tm - SUBLANES:]
        hs_ref[0] = hs[tm - SUBLANES:]


def _lru(x, g, win, cw, cb, wax, bax, lam, wout, batch=None, seq=None, sample_state=None):
    t, d = x.shape
    w = wout.shape[0]
    bw = w // LRU_BLOCKS
    sample = sample_state is not None
    const2 = lambda *_: (0, 0)
    const3 = lambda *_: (0, 0, 0)
    single = pl.Buffered(1)
    if sample:
        tm = t
        grid = (1,)
        row = lambda i: (0, 0)
        dims = ("arbitrary",)
    else:
        tm = 256
        nt = seq // tm
        grid = (batch, nt)
        row = lambda b, i: (b * nt + i, 0)
        dims = ("arbitrary", "arbitrary")
    in_specs = [
        pl.BlockSpec((tm, d), row),
        pl.BlockSpec((1, d), const2),
        pl.BlockSpec((d, 2 * w), const2, pipeline_mode=single),
        pl.BlockSpec((CONV_WIDTH, w), const2),
        pl.BlockSpec((1, w), const2),
        pl.BlockSpec((LRU_BLOCKS, bw, 2 * bw), const3),
        pl.BlockSpec((LRU_BLOCKS, 1, 2 * bw), const3),
        pl.BlockSpec((1, w), const2),
        pl.BlockSpec((w, d), const2, pipeline_mode=single),
    ]
    args = [x, g.reshape(1, d), win, cw, cb.reshape(1, w), wax, bax, lam.reshape(1, w), wout]
    scratch = [pltpu.VMEM((tm, w), F32), pltpu.VMEM((tm, w), F32)]
    if sample:
        in_specs += [pl.BlockSpec((tm, w), row)] * 4
        args += list(sample_state)
        out_specs = [pl.BlockSpec((tm, d), row), pl.BlockSpec((tm, w), row), pl.BlockSpec((tm, w), row)]
        out_shape = [jax.ShapeDtypeStruct((t, d), F32), jax.ShapeDtypeStruct((t, w), F32),
                     jax.ShapeDtypeStruct((t, w), F32)]
    else:
        tail = lambda b, i: (b, 0, 0)
        out_specs = [pl.BlockSpec((tm, d), row), pl.BlockSpec((1, SUBLANES, w), tail),
                     pl.BlockSpec((1, SUBLANES, w), tail)]
        out_shape = [jax.ShapeDtypeStruct((t, d), F32), jax.ShapeDtypeStruct((batch, SUBLANES, w), F32),
                     jax.ShapeDtypeStruct((batch, SUBLANES, w), F32)]
        scratch += [pltpu.VMEM((SUBLANES, w), F32), pltpu.VMEM((SUBLANES, w), F32)]
    return pl.pallas_call(
        functools.partial(_lru_body, sample=sample, tm=tm, width=w),
        grid=grid,
        in_specs=in_specs,
        out_specs=out_specs,
        out_shape=out_shape,
        scratch_shapes=scratch,
        compiler_params=pltpu.CompilerParams(
            dimension_semantics=dims,
            vmem_limit_bytes=_vmem_limit(40 * 1024 * 1024)),
        name="lru_sample" if sample else "lru_prompt",
    )(*args)


def kernel(x_prompt, x_sample, cache_k, cache_v, cache_kidx, page_table, state_conv, state_h, rel_bias, attn_w_in, attn_w_out, lru_w_in, lru_conv_w, lru_conv_b, lru_w_a, lru_b_a, lru_w_x, lru_b_x, lru_lambda, lru_w_out, norm_g, ffn_w_gu, ffn_w_down, final_norm_g):
    batch, seq, d = x_prompt.shape
    n_dec, dec_seq, _ = x_sample.shape
    n_pages = page_table.shape[1]
    n_pool = cache_k.shape[0]
    a = ATTN_WIDTH
    w = lru_w_out.shape[0]
    assert seq % SCORE_CHUNK == 0 and dec_seq == SUBLANES and page_table.shape[1] >= 2

    xp = x_prompt.reshape(batch * seq, d)
    xs = x_sample.reshape(n_dec * dec_seq, d)

    wgu = ffn_w_gu.astype(BF16)
    wdn = ffn_w_down.astype(BF16)
    qi_end = 3 * a + IDX_HEADS * IDX_DIM
    wqkv = attn_w_in[:, :3 * a].astype(BF16)
    wqi = attn_w_in[:, 3 * a:qi_end].astype(BF16)
    wkw = jnp.pad(attn_w_in[:, qi_end:], ((0, 0), (0, LANES - (IDX_DIM + IDX_HEADS)))).astype(BF16)
    wo = attn_w_out.astype(BF16)
    lwin = lru_w_in.astype(BF16)
    lwout = lru_w_out.astype(BF16)
    wax = jnp.concatenate([lru_w_a, lru_w_x], axis=-1).astype(BF16)
    bax = jnp.concatenate([lru_b_a, lru_b_x], axis=-1)[:, None, :]

    xp = _ffn(xp, norm_g[0, 0], wgu[0, 0], wdn[0, 0])
    xs = _ffn(xs, norm_g[0, 0], wgu[0, 0], wdn[0, 0])

    q_p, kt_p, vt_p, ktb_p, vb_p, qi_p, kw_p, kwt_p = _attn_proj(
        xp, norm_g[0, 1], wqkv, wqi, wkw, HEAD_DIM ** -0.5 * LOG2E, batch=batch)
    q_s, k_s, v_s, _, _, qi_s, kw_s = _attn_proj(xs, norm_g[0, 1], wqkv, wqi, wkw, HEAD_DIM ** -0.5)

    nqb = seq // Q_BLOCK
    qi4 = qi_p.reshape(batch, nqb, Q_BLOCK, IDX_HEADS, IDX_DIM).transpose(0, 1, 3, 2, 4)
    qi4 = qi4.reshape(batch, nqb, IDX_HEADS * Q_BLOCK, IDX_DIM)
    kit = kwt_p[:, :IDX_DIM, :].astype(BF16)
    kt = jnp.pad(ktb_p, ((0, 0), (0, 0), (0, Q_BLOCK)))
    vb3 = jnp.pad(vb_p.reshape(batch, seq, a), ((0, 0), (0, Q_BLOCK), (0, 0)))
    i_q = np.arange(Q_BLOCK)[:, None]
    c_k = np.arange(3 * Q_BLOCK)[None, :]
    toe = _bias_rows(rel_bias, Q_BLOCK + i_q - c_k)
    toe = toe.reshape(N_PAIRS, 2 * Q_BLOCK, 3 * Q_BLOCK) * LOG2E
    xp = _attn_prompt(xp, q_p, kt, vb3, qi4, kw_p, kit, toe, wo, batch, seq)

    qi_s3 = qi_s.reshape(n_dec, dec_seq, IDX_HEADS, IDX_DIM).transpose(0, 2, 1, 3)
    qi_s3 = qi_s3.reshape(n_dec, IDX_HEADS * dec_seq, IDX_DIM)
    wi_s = kw_s[:, IDX_DIM:IDX_DIM + IDX_HEADS].reshape(n_dec, dec_seq, IDX_HEADS) * (IDX_HEADS ** -0.5)
    wb = jnp.broadcast_to(wi_s.transpose(0, 2, 1).reshape(n_dec, IDX_HEADS * dec_seq, 1),
                          (n_dec, IDX_HEADS * dec_seq, LANES))
    new_keys = PAGE_SIZE - dec_seq
    ki_new = jnp.pad(kw_s[:, :IDX_DIM].reshape(n_dec, dec_seq, IDX_DIM).transpose(0, 2, 1),
                     ((0, 0), (0, 0), (0, new_keys)))
    scores_past, scores_new = _sample_scores(page_table, qi_s3, wb, cache_kidx.transpose(0, 2, 1), ki_new, dec_seq)
    new_pad = ((0, 0), (0, 0), (0, 0), (0, new_keys))
    k_new = jnp.pad(k_s.reshape(n_dec, dec_seq, N_HEADS, HEAD_DIM).transpose(0, 2, 3, 1), new_pad)
    v_new = jnp.pad(v_s.reshape(n_dec, dec_seq, N_HEADS, HEAD_DIM).transpose(0, 2, 3, 1), new_pad)
    tok = np.arange(dec_seq)[:, None]
    jj = np.arange(PAGE_SIZE)[None, :]
    b_last = _bias_rows(rel_bias, PAGE_SIZE + tok - jj)
    b_new = _bias_rows(rel_bias, tok - jj)
    bias3 = jnp.stack([jnp.zeros_like(b_last), b_last, b_new]).reshape(3, N_HEADS * dec_seq, PAGE_SIZE)
    xs3 = _attn_sample(page_table, scores_past, scores_new, q_s.astype(F32).reshape(n_dec, dec_seq, a),
                       cache_k.transpose(0, 2, 3, 1), cache_v.transpose(0, 2, 3, 1), k_new, v_new, bias3, xs.reshape(n_dec, dec_seq, d), wo, dec_seq)
    xs = xs3.reshape(n_dec * dec_seq, d)

    xp = _ffn(xp, norm_g[0, 2], wgu[0, 1], wdn[0, 1])
    xs = _ffn(xs, norm_g[0, 2], wgu[0, 1], wdn[0, 1])

    xp = _ffn(xp, norm_g[1, 0], wgu[1, 0], wdn[1, 0])
    xs = _ffn(xs, norm_g[1, 0], wgu[1, 0], wdn[1, 0])

    lru_args = (lwin, lru_conv_w, lru_conv_b, wax, bax, lru_lambda, lwout)
    xp, conv_p8, h_p8 = _lru(xp, norm_g[1, 1], *lru_args, batch=batch, seq=seq)
    firsts = []
    for k in range(1, CONV_WIDTH):
        r = jnp.pad(state_conv[:, CONV_WIDTH - 1 - k:, :], ((0, 0), (0, dec_seq - k), (0, 0)))
        firsts.append(r.reshape(n_dec * dec_seq, w))
    h0 = jnp.broadcast_to(state_h[:, None, :], (n_dec, dec_seq, w)).reshape(n_dec * dec_seq, w)
    xs, xr_s, hs_s = _lru(xs, norm_g[1, 1], *lru_args, sample_state=firsts + [h0])

    xp = _ffn(xp, norm_g[1, 2], wgu[1, 1], wdn[1, 1], gf=final_norm_g)
    xs = _ffn(xs, norm_g[1, 2], wgu[1, 1], wdn[1, 1], gf=final_norm_g)

    nh, hd = N_HEADS, HEAD_DIM
    keep = CONV_WIDTH - 1
    return (
        xp.reshape(batch, seq, d),
        xs.reshape(n_dec, dec_seq, d),
        kt_p.reshape(batch, nh, hd, seq).transpose(0, 3, 1, 2),
        vt_p.reshape(batch, nh, hd, seq).transpose(0, 3, 1, 2),
        kwt_p[:, :IDX_DIM, :].transpose(0, 2, 1),
        k_s.reshape(n_dec, dec_seq, nh, hd),
        v_s.reshape(n_dec, dec_seq, nh, hd),
        kw_s[:, :IDX_DIM].reshape(n_dec, dec_seq, IDX_DIM),
        conv_p8[:, SUBLANES - keep:, :],
        h_p8[:, SUBLANES - 1, :],
        xr_s.reshape(n_dec, dec_seq, w)[:, dec_seq - keep:, :],
        hs_s.reshape(n_dec, dec_seq, w)[:, dec_seq - 1, :],
    )
```
